```python
import jax, jax.numpy as jnp
from jax import lax
import numpy as np

D_MODEL = 2048
BATCH = 16
SEQ = 2048
DEPTH = 4

N_A_LAYERS = DEPTH // 2
N_B_LAYERS = DEPTH - N_A_LAYERS
HGRN_HEAD_DIM = 128
HGRN_HEADS = D_MODEL // HGRN_HEAD_DIM
HGRN_VDIM = D_MODEL // HGRN_HEADS
HGRN_CHUNK = 32
ATTN_GROUPS = ((128, 1), (512, 4), (2048, 16))
N_GROUPS = len(ATTN_GROUPS)
ATTN_HEAD_DIM = 128
ATTN_HEADS = D_MODEL // 256
ATTN_WIDTH = ATTN_HEADS * ATTN_HEAD_DIM
D_FF = ((8 * D_MODEL // 3 + 255) // 256) * 256
EPS = 1e-6
MASK_VALUE = -1e30
LB_FLOOR = 1e-30

kernel_name = 'yoco_hgrn2_dilated_attn_macaron_adaln'


def rms_norm(x, gain):
    xf = x.astype(jnp.float32)
    y = xf * lax.rsqrt(jnp.mean(xf * xf, axis=-1, keepdims=True) + EPS)
    return (y * gain.astype(jnp.float32)).astype(x.dtype)


def modulate(x, shift, scale):
    return x * (1 + scale[:, None, :]) + shift[:, None, :]


def swiglu(h, w_in, w_out):
    gate, up = jnp.split(h @ w_in, 2, axis=-1)
    return (jax.nn.silu(gate) * up) @ w_out


def hgrn2_mixer(h, w_in, w_out, lb, out_gain):
    B, S, D = h.shape
    H, dk, dv, C = HGRN_HEADS, HGRN_HEAD_DIM, HGRN_VDIM, HGRN_CHUNK
    q, f, v, g = jnp.split(h @ w_in, 4, axis=-1)
    lb = lb.reshape(H, dk)
    ff = f.astype(jnp.float32).reshape(B, S, H, dk)
    log_f = jnp.logaddexp(jnp.log(jnp.maximum(lb, LB_FLOOR)), jnp.log1p(-lb) + jax.nn.log_sigmoid(ff))
    k = (1 - lb) * jax.nn.sigmoid(-ff)
    q = jax.nn.silu(q.astype(jnp.float32)).reshape(B, S, H, dk)
    v = v.astype(jnp.float32).reshape(B, S, H, dv)
    n = S // C

    def to_chunks(t):
        return t.reshape(B, n, C, H, t.shape[-1]).transpose(1, 0, 3, 2, 4)

    qc, kc, vc, lfc = to_chunks(q), to_chunks(k), to_chunks(v), to_chunks(log_f)
    Gc = jnp.cumsum(lfc, axis=3)
    causal = jnp.tril(jnp.ones((C, C), dtype=bool))[:, :, None]

    def step(state, inp):
        q_, k_, v_, G_ = inp
        diff = G_[:, :, :, None, :] - G_[:, :, None, :, :]
        decay = jnp.where(causal, jnp.exp(jnp.where(causal, diff, 0.0)), 0.0)
        A = jnp.einsum('bhtsd,bhsd->bhts', q_[:, :, :, None, :] * decay, k_)
        o = (jnp.einsum('bhts,bhsv->bhtv', A, v_)
             + jnp.einsum('bhtd,bhdv->bhtv', q_ * jnp.exp(G_), state))
        G_last = G_[:, :, -1:, :]
        state = (state * jnp.exp(G_last)[:, :, 0, :, None]
                 + jnp.einsum('bhsd,bhsv->bhdv', k_ * jnp.exp(G_last - G_), v_))
        return state, o

    state0 = jnp.zeros((B, H, dk, dv), jnp.float32)
    _, o = lax.scan(step, state0, (qc, kc, vc, Gc))
    o = o.transpose(1, 0, 3, 2, 4).reshape(B, S, H, dv)
    o = rms_norm(o, out_gain) * jax.nn.silu(g.astype(jnp.float32)).reshape(B, S, H, dv)
    return o.reshape(B, S, H * dv).astype(h.dtype) @ w_out


def dilated_group(q, k, v, window, dil):
    B, S, H, Dh = q.shape
    n_back = window // dil
    blk = n_back
    L = S // dil
    nb = -(-L // blk)
    Lp = nb * blk

    def to_res(t):
        return t.reshape(B, L, dil, H, Dh).transpose(0, 2, 1, 3, 4)

    qb = jnp.pad(to_res(q), ((0, 0), (0, 0), (0, Lp - L), (0, 0), (0, 0))).reshape(B, dil, nb, blk, H, Dh)

    def key_blocks(t):
        t = jnp.pad(to_res(t), ((0, 0), (0, 0), (blk, Lp - L), (0, 0), (0, 0))).reshape(B, dil, nb + 1, blk, H, Dh)
        return jnp.concatenate([t[:, :, :-1], t[:, :, 1:]], axis=3)

    kb, vb = key_blocks(k), key_blocks(v)
    s = jnp.einsum('brnqhd,brnkhd->brnhqk', qb, kb).astype(jnp.float32) * (Dh ** -0.5)
    rel = jnp.arange(2 * blk)[None, :] - blk - jnp.arange(blk)[:, None]
    band = (rel <= 0) & (rel >= -n_back)
    valid = (jnp.arange(nb)[:, None] * blk + jnp.arange(2 * blk)[None, :] - blk) >= 0
    mask = band[None] & valid[:, None, :]
    s = jnp.where(mask[:, None], s, MASK_VALUE)
    lse = jax.nn.logsumexp(s, axis=-1)
    p = jnp.where(mask[:, None], jnp.exp(s - lse[..., None]), 0.0)
    o = jnp.einsum('brnhqk,brnkhd->brnqhd', p.astype(vb.dtype), vb)
    o = o.reshape(B, dil, Lp, H, Dh)[:, :, :L].transpose(0, 2, 1, 3, 4).reshape(B, S, H, Dh)
    lse = lse.transpose(0, 1, 2, 4, 3).reshape(B, dil, Lp, H)[:, :, :L].transpose(0, 2, 1, 3).reshape(B, S, H)
    return o, lse


def dilated_attention_mixer(h, w_q, q_gain, k_sh, v_sh, w_o):
    B, S, _ = h.shape
    q = rms_norm((h @ w_q).reshape(B, S, N_GROUPS, ATTN_HEADS, ATTN_HEAD_DIM), q_gain[:, None, :])
    outs, lses = [], []
    for gi, (win, dil) in enumerate(ATTN_GROUPS):
        o, lse = dilated_group(q[:, :, gi], k_sh[:, :, gi], v_sh[:, :, gi], win, dil)
        outs.append(o)
        lses.append(lse)
    w = jax.nn.softmax(jnp.stack(lses), axis=0)
    o = jnp.sum(w[..., None] * jnp.stack(outs).astype(jnp.float32), axis=0).astype(h.dtype)
    return o.reshape(B, S, ATTN_WIDTH) @ w_o


def shared_kv(x, silu_c, norm_g, w_ada, b_ada, w_kv, k_gain):
    B, S, _ = x.shape
    shift, scale = jnp.split(silu_c @ w_ada + b_ada, 2, axis=-1)
    hn = modulate(rms_norm(x, norm_g), shift, scale)
    k, v = jnp.split(hn @ w_kv, 2, axis=-1)
    k = rms_norm(k.reshape(B, S, N_GROUPS, ATTN_HEADS, ATTN_HEAD_DIM), k_gain[:, None, :])
    v = v.reshape(B, S, N_GROUPS, ATTN_HEADS, ATTN_HEAD_DIM)
    return k, v


def setup_inputs(seed: int = 0) -> dict:
    key = jax.random.key(seed)
    ks = jax.random.split(key, 20)
    D = D_MODEL
    nrm = lambda k, shape, s: jax.random.normal(k, shape, jnp.float32) * s
    GW = N_GROUPS * ATTN_WIDTH
    return {
        'x': nrm(ks[0], (BATCH, SEQ, D), 1.0),
        'c': nrm(ks[1], (BATCH, D), 1.0),
        'norm_g': 1.0 + nrm(ks[2], (DEPTH, 3, D), 0.02),
        'w_ada': nrm(ks[3], (DEPTH, D, 9 * D), 0.1 * D ** -0.5),
        'b_ada': nrm(ks[4], (DEPTH, 9 * D), 0.02),
        'w_ffn_in': nrm(ks[5], (DEPTH, 2, D, 2 * D_FF), D ** -0.5),
        'w_ffn_out': nrm(ks[6], (DEPTH, 2, D_FF, D), D_FF ** -0.5),
        'hgrn_w_in': nrm(ks[7], (N_A_LAYERS, D, 4 * D), D ** -0.5),
        'hgrn_w_out': nrm(ks[8], (N_A_LAYERS, D, D), D ** -0.5),
        'hgrn_lb_logits': nrm(ks[9], (N_A_LAYERS, D), 1.0),
        'hgrn_out_gain': 1.0 + nrm(ks[10], (N_A_LAYERS, HGRN_VDIM), 0.02),
        'kv_norm_g': 1.0 + nrm(ks[11], (D,), 0.02),
        'kv_w_ada': nrm(ks[12], (D, 2 * D), 0.1 * D ** -0.5),
        'kv_b_ada': nrm(ks[13], (2 * D,), 0.02),
        'w_kv': nrm(ks[14], (D, 2 * GW), D ** -0.5),
        'k_gain': 1.0 + nrm(ks[15], (N_GROUPS, ATTN_HEAD_DIM), 0.02),
        'attn_w_q': nrm(ks[16], (N_B_LAYERS, D, GW), D ** -0.5),
        'attn_q_gain': 1.0 + nrm(ks[17], (N_B_LAYERS, N_GROUPS, ATTN_HEAD_DIM), 0.02),
        'attn_w_o': nrm(ks[18], (N_B_LAYERS, ATTN_WIDTH, D), ATTN_WIDTH ** -0.5),
    }


def reference(x, c, norm_g, w_ada, b_ada, w_ffn_in, w_ffn_out, hgrn_w_in, hgrn_w_out,
              hgrn_lb_logits, hgrn_out_gain, kv_norm_g, kv_w_ada, kv_b_ada, w_kv, k_gain,
              attn_w_q, attn_q_gain, attn_w_o):
    B, S, D = x.shape
    silu_c = jax.nn.silu(c)
    p = jax.nn.softmax(hgrn_lb_logits.astype(jnp.float32), axis=0)
    lower_bounds = jnp.cumsum(p, axis=0) - p[0]
    k_sh = v_sh = None
    for l in range(DEPTH):
        m = (silu_c @ w_ada[l] + b_ada[l]).reshape(B, 3, 3, D)
        hn = modulate(rms_norm(x, norm_g[l, 0]), m[:, 0, 0], m[:, 0, 1])
        x = x + 0.5 * (1 + m[:, 0, 2])[:, None, :] * swiglu(hn, w_ffn_in[l, 0], w_ffn_out[l, 0])
        hn = modulate(rms_norm(x, norm_g[l, 1]), m[:, 1, 0], m[:, 1, 1])
        if l < N_A_LAYERS:
            y = hgrn2_mixer(hn, hgrn_w_in[l], hgrn_w_out[l], lower_bounds[l], hgrn_out_gain[l])
        else:
            j = l - N_A_LAYERS
            y = dilated_attention_mixer(hn, attn_w_q[j], attn_q_gain[j], k_sh, v_sh, attn_w_o[j])
        x = x + (1 + m[:, 1, 2])[:, None, :] * y
        hn = modulate(rms_norm(x, norm_g[l, 2]), m[:, 2, 0], m[:, 2, 1])
        x = x + 0.5 * (1 + m[:, 2, 2])[:, None, :] * swiglu(hn, w_ffn_in[l, 1], w_ffn_out[l, 1])
        if l == N_A_LAYERS - 1:
            k_sh, v_sh = shared_kv(x, silu_c, kv_norm_g, kv_w_ada, kv_b_ada, w_kv, k_gain)
    return x
```

```python
import functools

import jax
import jax.numpy as jnp
from jax import lax
from jax.experimental import pallas as pl
from jax.experimental.pallas import tpu as pltpu

F32 = jnp.float32
BF16 = jnp.bfloat16

EPS = 1e-6
MASK_VALUE = -1e30
LB_FLOOR = 1e-30
HEAD_DIM = 128
ATTN_GROUPS = ((128, 1), (512, 4), (2048, 16))
ATTN_BLOCK = 128
HGRN_CHUNK = 128
HGRN_DIAG = 16

VMEM_LIMIT_BYTES = 56 * 1024 * 1024


def _pick(n, target, mult=128):
    if n <= target:
        return n
    d = (target // mult) * mult
    while d >= mult:
        if n % d == 0:
            return d
        d -= mult
    raise ValueError(f"no tile for {n}")


def _params(sem):
    return pltpu.CompilerParams(dimension_semantics=sem, vmem_limit_bytes=VMEM_LIMIT_BYTES)


def _silu(x):
    return x * jax.nn.sigmoid(x)


def _dot(a, b):
    return jnp.dot(a, b, preferred_element_type=F32)


def _dot_nt(a, b):
    return lax.dot_general(a, b, (((1,), (1,)), ((), ())), preferred_element_type=F32)


def _norm_mod(x, gain, shift, scale):
    ms = jnp.mean(x * x, axis=-1, keepdims=True)
    y = x * lax.rsqrt(ms + EPS) * gain
    return y * (1 + scale) + shift


def _head_norm(a, gain):
    outs = []
    for h in range(a.shape[1] // HEAD_DIM):
        ah = a[:, h * HEAD_DIM:(h + 1) * HEAD_DIM]
        ms = jnp.mean(ah * ah, axis=-1, keepdims=True)
        outs.append(ah * lax.rsqrt(ms + EPS) * gain)
    return outs[0] if len(outs) == 1 else jnp.concatenate(outs, axis=1)


def _ada_kernel(c_ref, w_ref, b_ref, o_ref):
    sc = _silu(c_ref[...]).astype(BF16)
    o_ref[0] = _dot(sc, w_ref[0].astype(BF16)) + b_ref[0]


def _ada(c, w, b):
    L, D, N = w.shape
    B = c.shape[0]
    tn = _pick(N, 1024)
    return pl.pallas_call(
        _ada_kernel,
        grid=(L, N // tn),
        in_specs=[
            pl.BlockSpec((B, D), lambda l, j: (0, 0)),
            pl.BlockSpec((1, D, tn), lambda l, j: (l, 0, j)),
            pl.BlockSpec((1, 1, tn), lambda l, j: (l, 0, j)),
        ],
        out_specs=pl.BlockSpec((1, B, tn), lambda l, j: (l, 0, j)),
        out_shape=jax.ShapeDtypeStruct((L, B, N), F32),
        compiler_params=_params(("parallel", "parallel")),
        name="ada",
    )(c, w, b.reshape(L, 1, N))


def _ffn_kernel(x_ref, g_ref, mod_ref, wg_ref, wu_ref, wo_ref, o_ref, hn_ref, *, sub, nj):
    j = pl.program_id(1)

    @pl.when(j == 0)
    def _():
        hn_ref[...] = _norm_mod(
            x_ref[...], g_ref[...], mod_ref[0, 3 * sub:3 * sub + 1, :],
            mod_ref[0, 3 * sub + 1:3 * sub + 2, :]).astype(BF16)

    hn = hn_ref[...]
    hg = _dot(hn, wg_ref[...])
    hu = _dot(hn, wu_ref[...])
    act = (_silu(hg) * hu).astype(BF16)
    y = _dot(act, wo_ref[...])

    @pl.when(j == 0)
    def _():
        o_ref[...] = y

    @pl.when(j > 0)
    def _():
        o_ref[...] += y

    @pl.when(j == nj - 1)
    def _():
        gate = mod_ref[0, 3 * sub + 2:3 * sub + 3, :]
        o_ref[...] = x_ref[...] + (0.5 * (1 + gate)) * o_ref[...]


def _ffn(x, gain, mod, w_in, w_out, sub, seq):
    M, D = x.shape
    F = w_out.shape[0]
    tm = _pick(seq, 512, 8)
    tf = _pick(F, 512)
    nj = F // tf
    per_b = seq // tm
    return pl.pallas_call(
        functools.partial(_ffn_kernel, sub=sub, nj=nj),
        grid=(M // tm, nj),
        in_specs=[
            pl.BlockSpec((tm, D), lambda i, j: (i, 0)),
            pl.BlockSpec((1, D), lambda i, j: (0, 0)),
            pl.BlockSpec((1, 9, D), lambda i, j: (i // per_b, 0, 0)),
            pl.BlockSpec((D, tf), lambda i, j: (0, j)),
            pl.BlockSpec((D, tf), lambda i, j: (0, nj + j)),
            pl.BlockSpec((tf, D), lambda i, j: (j, 0)),
        ],
        out_specs=pl.BlockSpec((tm, D), lambda i, j: (i, 0)),
        out_shape=jax.ShapeDtypeStruct((M, D), F32),
        scratch_shapes=[pltpu.VMEM((tm, D), BF16)],
        compiler_params=_params(("parallel", "arbitrary")),
        name="ffn",
    )(x, gain, mod, w_in, w_in, w_out)


def _proj_kernel(*refs, n_w, n_extra, shift_row, scale_row, epilogue):
    x_ref, g_ref, mod_ref = refs[:3]
    w_refs = refs[3:3 + n_w]
    extra_refs = refs[3 + n_w:3 + n_w + n_extra]
    out_refs = refs[3 + n_w + n_extra:-1]
    hn_ref = refs[-1]

    @pl.when(pl.program_id(1) == 0)
    def _():
        hn_ref[...] = _norm_mod(
            x_ref[...], g_ref[...], mod_ref[0, shift_row:shift_row + 1, :],
            mod_ref[0, scale_row:scale_row + 1, :]).astype(BF16)

    hn = hn_ref[...]
    accs = [_dot(hn, w[...]) for w in w_refs]
    epilogue(accs, extra_refs, out_refs)


def _proj(x, gain, mod, w, col_offsets, n_cols, tn, tm, seq, shift_row, scale_row,
          extras, extra_specs, out_dtypes, epilogue, name):
    M, D = x.shape
    per_b = seq // tm
    n_mod = mod.shape[1]
    w_specs = [
        pl.BlockSpec((D, tn), functools.partial(lambda i, j, o: (0, o + j), o=off // tn))
        for off in col_offsets
    ]
    return pl.pallas_call(
        functools.partial(_proj_kernel, n_w=len(col_offsets), n_extra=len(extras),
                          shift_row=shift_row, scale_row=scale_row, epilogue=epilogue),
        grid=(M // tm, n_cols // tn),
        in_specs=[
            pl.BlockSpec((tm, D), lambda i, j: (i, 0)),
            pl.BlockSpec((1, D), lambda i, j: (0, 0)),
            pl.BlockSpec((1, n_mod, D), lambda i, j: (i // per_b, 0, 0)),
        ] + w_specs + extra_specs,
        out_specs=[pl.BlockSpec((tm, tn), lambda i, j: (i, j)) for _ in out_dtypes],
        out_shape=[jax.ShapeDtypeStruct((M, n_cols), dt) for dt in out_dtypes],
        scratch_shapes=[pltpu.VMEM((tm, D), BF16)],
        compiler_params=_params(("parallel", "arbitrary")),
        name=name,
    )(x, gain, mod, *([w] * len(col_offsets)), *extras)


def _hgrn_epilogue(accs, extra_refs, out_refs):
    q, ff, v, g = accs
    log_lb, log_1m_lb, one_m_lb = (r[...] for r in extra_refs)
    q_ref, lf_ref, k_ref, v_ref, sg_ref = out_refs
    q_ref[...] = _silu(q)
    lf_ref[...] = jnp.logaddexp(log_lb, log_1m_lb + jax.nn.log_sigmoid(ff))
    k_ref[...] = one_m_lb * jax.nn.sigmoid(-ff)
    v_ref[...] = v.astype(BF16)
    sg_ref[...] = _silu(g).astype(BF16)


def _kv_epilogue(accs, extra_refs, out_refs):
    k, v = accs
    out_refs[0][...] = _head_norm(k, extra_refs[0][0]).astype(BF16)
    out_refs[1][...] = v.astype(BF16)


def _q_epilogue(accs, extra_refs, out_refs):
    out_refs[0][...] = _head_norm(accs[0], extra_refs[0][0]).astype(BF16)


def _hgrn_kernel(q_ref, lf_ref, k_ref, v_ref, sg_ref, gain_ref, o_ref, st_ref, *, n_chunks):
    C, c = HGRN_CHUNK, HGRN_DIAG
    row = lax.broadcasted_iota(jnp.int32, (C, C), 0)
    col = lax.broadcasted_iota(jnp.int32, (C, C), 1)
    tri = (col <= row).astype(BF16)
    delta = row - col
    same_diag_block = (row // c) == (col // c)
    tok = lax.broadcasted_iota(jnp.int32, (C, 1), 0)
    gain = gain_ref[...]
    st_ref[...] = jnp.zeros_like(st_ref)

    def chunk(ci, carry):
        c0 = pl.multiple_of(ci * C, C)
        qs = q_ref[pl.ds(c0, C), :]
        lf = lf_ref[pl.ds(c0, C), :]
        k = k_ref[pl.ds(c0, C), :]
        v = v_ref[pl.ds(c0, C), :]
        lf_hi = lf.astype(BF16)
        lf_lo = (lf - lf_hi.astype(F32)).astype(BF16)
        G = _dot(tri, lf_hi) + _dot(tri, lf_lo)
        GT, qT, kT = G.T, qs.T, k.T

        A = jnp.zeros((C, C), F32)
        for d in range(c):
            if d == 0:
                w = qT * kT
            else:
                Gr = pltpu.roll(GT, C - d, axis=1)
                qr = pltpu.roll(qT, C - d, axis=1)
                w = qr * kT * jnp.exp(Gr - GT)
            diag = jnp.sum(w, axis=0, keepdims=True)
            A = jnp.where(delta == d, jnp.broadcast_to(diag, (C, C)), A)
        A = jnp.where(same_diag_block, A, 0.0)

        P = 2 * c
        while P <= C:
            half = P // 2
            Ge = jnp.concatenate(
                [jnp.broadcast_to(G[b * P + half - 1:b * P + half, :], (P, HEAD_DIM))
                 for b in range(C // P)], axis=0)
            second = (tok % P) >= half
            qe = jnp.where(second, qs * jnp.exp(G - Ge), 0.0).astype(BF16)
            ke = jnp.where(second, 0.0, k * jnp.exp(Ge - G)).astype(BF16)
            A = A + jnp.where((row // P) == (col // P), _dot_nt(qe, ke), 0.0)
            P *= 2

        S = st_ref[...]
        o = _dot(A.astype(BF16), v) + _dot((qs * jnp.exp(G)).astype(BF16), S.astype(BF16))
        g_last = GT[:, C - 1:C]
        k_dec = (kT * jnp.exp(g_last - GT)).astype(BF16)
        st_ref[...] = S * jnp.exp(g_last) + _dot(k_dec, v)

        ms = jnp.mean(o * o, axis=-1, keepdims=True)
        on = o * lax.rsqrt(ms + EPS) * gain
        o_ref[pl.ds(c0, C), :] = (on * sg_ref[pl.ds(c0, C), :].astype(F32)).astype(BF16)
        return carry

    lax.fori_loop(0, n_chunks, chunk, 0)


def _hgrn(q, lf, k, v, sg, out_gain, seq):
    M, D = q.shape
    blk = lambda: pl.BlockSpec((seq, HEAD_DIM), lambda b, h: (b, h))
    return pl.pallas_call(
        functools.partial(_hgrn_kernel, n_chunks=seq // HGRN_CHUNK),
        grid=(M // seq, D // HEAD_DIM),
        in_specs=[blk(), blk(), blk(), blk(), blk(),
                  pl.BlockSpec((1, HEAD_DIM), lambda b, h: (0, 0))],
        out_specs=blk(),
        out_shape=jax.ShapeDtypeStruct((M, D), BF16),
        scratch_shapes=[pltpu.VMEM((HEAD_DIM, HEAD_DIM), F32)],
        compiler_params=_params(("parallel", "parallel")),
        name="hgrn",
    )(q, lf, k, v, sg, out_gain)


def _oproj_kernel(a_ref, w_ref, x_ref, mod_ref, o_ref, *, gate_row):
    gate = mod_ref[0, gate_row:gate_row + 1, :]
    o_ref[...] = x_ref[...] + (1 + gate) * _dot(a_ref[...], w_ref[...])


def _oproj(a, w, x, mod, gate_row, seq):
    M, K = a.shape
    D = w.shape[1]
    tm = _pick(seq, 1024, 8)
    tn = _pick(D, 1024)
    per_b = seq // tm
    return pl.pallas_call(
        functools.partial(_oproj_kernel, gate_row=gate_row),
        grid=(M // tm, D // tn),
        in_specs=[
            pl.BlockSpec((tm, K), lambda i, j: (i, 0)),
            pl.BlockSpec((K, tn), lambda i, j: (0, j)),
            pl.BlockSpec((tm, tn), lambda i, j: (i, j)),
            pl.BlockSpec((1, 9, tn), lambda i, j: (i // per_b, 0, j)),
        ],
        out_specs=pl.BlockSpec((tm, tn), lambda i, j: (i, j)),
        out_shape=jax.ShapeDtypeStruct((M, D), F32),
        compiler_params=_params(("parallel", "arbitrary")),
        name="oproj",
    )(a, w, x, mod)


def _attn_kernel(*refs, n_heads, has_prev):
    if has_prev:
        q_ref, kc_ref, kp_ref, vc_ref, vp_ref, o_ref, st_ref = refs
    else:
        q_ref, kc_ref, vc_ref, o_ref, st_ref = refs
    T = ATTN_BLOCK
    row = lax.broadcasted_iota(jnp.int32, (T, T), 0)
    col = lax.broadcasted_iota(jnp.int32, (T, T), 1)
    lane = lax.broadcasted_iota(jnp.int32, (T, HEAD_DIM), 1)
    mask_c = col <= row
    if has_prev:
        mask_p = (col >= row) & (pl.program_id(2) > 0)
    scale = HEAD_DIM ** -0.5
    stats = jnp.zeros((T, HEAD_DIM), F32)
    for h in range(n_heads):
        sl = slice(h * HEAD_DIM, (h + 1) * HEAD_DIM)
        q = q_ref[0, :, sl]
        s_c = jnp.where(mask_c, _dot_nt(q, kc_ref[0, :, sl]) * scale, MASK_VALUE)
        m = jnp.max(s_c, axis=-1, keepdims=True)
        if has_prev:
            s_p = jnp.where(mask_p, _dot_nt(q, kp_ref[0, :, sl]) * scale, MASK_VALUE)
            m = jnp.maximum(m, jnp.max(s_p, axis=-1, keepdims=True))
        p_c = jnp.where(mask_c, jnp.exp(s_c - m), 0.0)
        l = jnp.sum(p_c, axis=-1, keepdims=True)
        o = _dot(p_c.astype(BF16), vc_ref[0, :, sl])
        if has_prev:
            p_p = jnp.where(mask_p, jnp.exp(s_p - m), 0.0)
            l = l + jnp.sum(p_p, axis=-1, keepdims=True)
            o = o + _dot(p_p.astype(BF16), vp_ref[0, :, sl])
        o_ref[0, :, sl] = o * (1.0 / l)
        stats = jnp.where(lane == h, m + jnp.log(l), stats)
    st_ref[0] = stats


def _attn_group(q, k, v, gi, dil, batch, seq, n_heads):
    M, GW = q.shape
    W = n_heads * HEAD_DIM
    n_groups = GW // W
    L = seq // dil
    nb = L // ATTN_BLOCK
    has_prev = nb > 1
    view = lambda t: t.reshape(batch, L, dil * GW)
    cur = pl.BlockSpec((1, ATTN_BLOCK, W), lambda b, r, n: (b, n, r * n_groups + gi))
    prev = pl.BlockSpec((1, ATTN_BLOCK, W),
                        lambda b, r, n: (b, jnp.maximum(n - 1, 0), r * n_groups + gi))
    if has_prev:
        in_specs, args = [cur, cur, prev, cur, prev], (view(q), view(k), view(k), view(v), view(v))
    else:
        in_specs, args = [cur, cur, cur], (view(q), view(k), view(v))
    o, st = pl.pallas_call(
        functools.partial(_attn_kernel, n_heads=n_heads, has_prev=has_prev),
        grid=(batch, dil, nb),
        in_specs=in_specs,
        out_specs=[pl.BlockSpec((1, ATTN_BLOCK, W), lambda b, r, n: (b, n, r)),
                   pl.BlockSpec((1, ATTN_BLOCK, HEAD_DIM), lambda b, r, n: (b, n, r))],
        out_shape=[jax.ShapeDtypeStruct((batch, L, dil * W), F32),
                   jax.ShapeDtypeStruct((batch, L, dil * HEAD_DIM), F32)],
        compiler_params=_params(("parallel", "parallel", "arbitrary")),
        name=f"attn_g{gi}",
    )(*args)
    return o.reshape(M, W), st.reshape(M, HEAD_DIM)


def _merge_kernel(*refs, n_groups, n_heads):
    o_refs = refs[:n_groups]
    st_refs = refs[n_groups:2 * n_groups]
    out_ref = refs[-1]
    sts = [r[...] for r in st_refs]
    for h in range(n_heads):
        sl = slice(h * HEAD_DIM, (h + 1) * HEAD_DIM)
        lses = [s[:, h:h + 1] for s in sts]
        mx = functools.reduce(jnp.maximum, lses)
        ws = [jnp.exp(l - mx) for l in lses]
        den = functools.reduce(lambda a, b: a + b, ws)
        acc = sum(w * r[:, sl] for w, r in zip(ws, o_refs))
        out_ref[:, sl] = (acc / den).astype(BF16)


def _merge(outs, stats, n_heads):
    M, W = outs[0].shape
    tm = _pick(M, 512, 8)
    n = len(outs)
    return pl.pallas_call(
        functools.partial(_merge_kernel, n_groups=n, n_heads=n_heads),
        grid=(M // tm,),
        in_specs=[pl.BlockSpec((tm, W), lambda i: (i, 0))] * n
        + [pl.BlockSpec((tm, HEAD_DIM), lambda i: (i, 0))] * n,
        out_specs=pl.BlockSpec((tm, W), lambda i: (i, 0)),
        out_shape=jax.ShapeDtypeStruct((M, W), BF16),
        compiler_params=_params(("parallel",)),
        name="merge",
    )(*outs, *stats)


def kernel(x, c, norm_g, w_ada, b_ada, w_ffn_in, w_ffn_out, hgrn_w_in, hgrn_w_out, hgrn_lb_logits, hgrn_out_gain, kv_norm_g, kv_w_ada, kv_b_ada, w_kv, k_gain, attn_w_q, attn_q_gain, attn_w_o):
    B, S, D = x.shape
    depth = w_ada.shape[0]
    n_a = hgrn_w_in.shape[0]
    n_groups = k_gain.shape[0]
    width = attn_w_o.shape[1]
    n_heads = width // HEAD_DIM
    GW = n_groups * width
    M = B * S

    mods = _ada(c, w_ada, b_ada).reshape(depth, B, 9, D)
    kv_mod = _ada(c, kv_w_ada[None], kv_b_ada[None]).reshape(B, 2, D)

    p = jax.nn.softmax(hgrn_lb_logits.astype(F32), axis=0)
    lb = jnp.cumsum(p, axis=0) - p[0]
    log_lb = jnp.log(jnp.maximum(lb, LB_FLOOR))
    log_1m_lb = jnp.log1p(-lb)
    one_m_lb = 1 - lb

    w_ffn_in_b = w_ffn_in.astype(BF16)
    w_ffn_out_b = w_ffn_out.astype(BF16)
    hgrn_w_in_b = hgrn_w_in.astype(BF16)
    hgrn_w_out_b = hgrn_w_out.astype(BF16)
    w_kv_b = w_kv.astype(BF16)
    attn_w_q_b = attn_w_q.astype(BF16)
    attn_w_o_b = attn_w_o.astype(BF16)

    xf = x.reshape(M, D)
    k_sh = v_sh = None
    for l in range(depth):
        mod = mods[l]
        xf = _ffn(xf, norm_g[l, 0][None], mod, w_ffn_in_b[l, 0], w_ffn_out_b[l, 0], 0, S)
        if l < n_a:
            tn = _pick(D, 512)
            vec = pl.BlockSpec((1, tn), lambda i, j: (0, j))
            q, lf, k, v, sg = _proj(
                xf, norm_g[l, 1][None], mod, hgrn_w_in_b[l], [0, D, 2 * D, 3 * D], D, tn,
                _pick(S, 512, 8), S, 3, 4,
                [log_lb[l][None], log_1m_lb[l][None], one_m_lb[l][None]], [vec, vec, vec],
                [F32, F32, F32, BF16, BF16], _hgrn_epilogue, "hgrn_proj")
            a = _hgrn(q, lf, k, v, sg, hgrn_out_gain[l][None], S)
            xf = _oproj(a, hgrn_w_out_b[l], xf, mod, 5, S)
        else:
            jl = l - n_a
            tn = width
            gain_spec = pl.BlockSpec((1, 1, HEAD_DIM), lambda i, j: (j, 0, 0))
            (q,) = _proj(
                xf, norm_g[l, 1][None], mod, attn_w_q_b[jl], [0], GW, tn, _pick(S, 1024, 8), S,
                3, 4, [attn_q_gain[jl][:, None, :]], [gain_spec], [BF16], _q_epilogue, "q_proj")
            outs, stats = [], []
            for gi, (win, dil) in enumerate(ATTN_GROUPS):
                assert win // dil == ATTN_BLOCK
                o, st = _attn_group(q, k_sh, v_sh, gi, dil, B, S, n_heads)
                outs.append(o)
                stats.append(st)
            a = _merge(outs, stats, n_heads)
            xf = _oproj(a, attn_w_o_b[jl], xf, mod, 5, S)
        xf = _ffn(xf, norm_g[l, 2][None], mod, w_ffn_in_b[l, 1], w_ffn_out_b[l, 1], 2, S)
        if l == n_a - 1:
            tn = width
            gain_spec = pl.BlockSpec((1, 1, HEAD_DIM), lambda i, j: (j, 0, 0))
            k_sh, v_sh = _proj(
                xf, kv_norm_g[None], kv_mod, w_kv_b, [0, GW], GW, tn, _pick(S, 512, 8), S,
                0, 1, [k_gain[:, None, :]], [gain_spec], [BF16, BF16], _kv_epilogue, "kv_proj")
    return xf.reshape(B, S, D)
```

```python
import functools

import numpy as np
import jax
import jax.numpy as jnp
from jax import lax
from jax.experimental import pallas as pl
from jax.experimental.pallas import tpu as pltpu

F32 = jnp.float32
BF16 = jnp.bfloat16

EPS = 1e-6
MASK_VALUE = -1e30
LB_FLOOR = 1e-30
HEAD_DIM = 128
ATTN_GROUPS = ((128, 1), (512, 4), (2048, 16))
ATTN_BLOCK = 128
HGRN_CHUNK = 128
HGRN_DIAG = 4
HGRN_HEADS_PER_STEP = 4

VMEM_LIMIT_BYTES = 56 * 1024 * 1024


def _pick(n, target, mult=128):
    if n <= target:
        return n
    d = (target // mult) * mult
    while d >= mult:
        if n % d == 0:
            return d
        d -= mult
    raise ValueError(f"no tile for {n}")


def _params(sem):
    return pltpu.CompilerParams(dimension_semantics=sem, vmem_limit_bytes=VMEM_LIMIT_BYTES)


def _silu(x):
    return x * jax.nn.sigmoid(x)


def _dot(a, b):
    return jnp.dot(a, b, preferred_element_type=F32)


def _dot_nt(a, b):
    return lax.dot_general(a, b, (((1,), (1,)), ((), ())), preferred_element_type=F32)


def _norm_mod(x, gain, shift, scale):
    ms = jnp.mean(x * x, axis=-1, keepdims=True)
    y = x * lax.rsqrt(ms + EPS) * gain
    return y * (1 + scale) + shift


def _head_norm(a, gain):
    outs = []
    for h in range(a.shape[1] // HEAD_DIM):
        ah = a[:, h * HEAD_DIM:(h + 1) * HEAD_DIM]
        ms = jnp.mean(ah * ah, axis=-1, keepdims=True)
        outs.append(ah * lax.rsqrt(ms + EPS) * gain)
    return outs[0] if len(outs) == 1 else jnp.concatenate(outs, axis=1)


def _ada_kernel(c_ref, w_ref, b_ref, o_ref):
    sc = _silu(c_ref[...]).astype(BF16)
    o_ref[0] = _dot(sc, w_ref[0].astype(BF16)) + b_ref[0]


def _ada(c, w, b):
    L, D, N = w.shape
    B = c.shape[0]
    tn = _pick(N, 1024)
    return pl.pallas_call(
        _ada_kernel,
        grid=(L, N // tn),
        in_specs=[
            pl.BlockSpec((B, D), lambda l, j: (0, 0)),
            pl.BlockSpec((1, D, tn), lambda l, j: (l, 0, j)),
            pl.BlockSpec((1, 1, tn), lambda l, j: (l, 0, j)),
        ],
        out_specs=pl.BlockSpec((1, B, tn), lambda l, j: (l, 0, j)),
        out_shape=jax.ShapeDtypeStruct((L, B, N), F32),
        compiler_params=_params(("parallel", "parallel")),
        name="ada",
    )(c, w, b.reshape(L, 1, N))


def _ffn_kernel(x_ref, g_ref, mod_ref, wg_ref, wu_ref, wo_ref, o_ref, hn_ref, *, sub, nj):
    j = pl.program_id(1)

    @pl.when(j == 0)
    def _():
        hn_ref[...] = _norm_mod(
            x_ref[...], g_ref[...], mod_ref[0, 3 * sub:3 * sub + 1, :],
            mod_ref[0, 3 * sub + 1:3 * sub + 2, :]).astype(BF16)
        o_ref[...] = jnp.zeros_like(o_ref)

    hn = hn_ref[...]
    hg = _dot(hn, wg_ref[...])
    hu = _dot(hn, wu_ref[...])
    act = (_silu(hg) * hu).astype(BF16)
    o_ref[...] += _dot(act, wo_ref[...])

    @pl.when(j == nj - 1)
    def _():
        gate = mod_ref[0, 3 * sub + 2:3 * sub + 3, :]
        o_ref[...] = x_ref[...] + (0.5 * (1 + gate)) * o_ref[...]


def _ffn(x, gain, mod, w_in, w_out, sub, seq):
    M, D = x.shape
    F = w_out.shape[0]
    tm = _pick(seq, 512, 8)
    tf = _pick(F, 512)
    nj = F // tf
    per_b = seq // tm
    return pl.pallas_call(
        functools.partial(_ffn_kernel, sub=sub, nj=nj),
        grid=(M // tm, nj),
        in_specs=[
            pl.BlockSpec((tm, D), lambda i, j: (i, 0)),
            pl.BlockSpec((1, D), lambda i, j: (0, 0)),
            pl.BlockSpec((1, 9, D), lambda i, j: (i // per_b, 0, 0)),
            pl.BlockSpec((D, tf), lambda i, j: (0, j)),
            pl.BlockSpec((D, tf), lambda i, j: (0, nj + j)),
            pl.BlockSpec((tf, D), lambda i, j: (j, 0)),
        ],
        out_specs=pl.BlockSpec((tm, D), lambda i, j: (i, 0)),
        out_shape=jax.ShapeDtypeStruct((M, D), F32),
        scratch_shapes=[pltpu.VMEM((tm, D), BF16)],
        compiler_params=_params(("parallel", "arbitrary")),
        name="ffn",
    )(x, gain, mod, w_in, w_in, w_out)


def _proj_kernel(*refs, n_w, n_extra, n_out, shift_row, scale_row, epilogue):
    x_ref, g_ref, mod_ref = refs[:3]
    w_refs = refs[3:3 + n_w]
    extra_refs = refs[3 + n_w:3 + n_w + n_extra]
    out_refs = refs[3 + n_w + n_extra:3 + n_w + n_extra + n_out]
    hn_ref = refs[3 + n_w + n_extra + n_out]
    scratch_refs = refs[4 + n_w + n_extra + n_out:]

    @pl.when(pl.program_id(1) == 0)
    def _():
        hn_ref[...] = _norm_mod(
            x_ref[...], g_ref[...], mod_ref[0, shift_row:shift_row + 1, :],
            mod_ref[0, scale_row:scale_row + 1, :]).astype(BF16)

    hn = hn_ref[...]
    accs = [_dot(hn, w[...]) for w in w_refs]
    epilogue(accs, extra_refs, out_refs, scratch_refs)


def _proj(x, gain, mod, w, col_offsets, n_cols, tn, tm, seq, shift_row, scale_row,
          extras, extra_specs, out_specs, out_shapes, epilogue, name, scratch=()):
    M, D = x.shape
    per_b = seq // tm
    n_mod = mod.shape[1]
    w_specs = [
        pl.BlockSpec((D, tn), functools.partial(lambda i, j, o: (0, o + j), o=off // tn))
        for off in col_offsets
    ]
    return pl.pallas_call(
        functools.partial(_proj_kernel, n_w=len(col_offsets), n_extra=len(extras),
                          n_out=len(out_specs), shift_row=shift_row, scale_row=scale_row,
                          epilogue=epilogue),
        grid=(M // tm, n_cols // tn),
        in_specs=[
            pl.BlockSpec((tm, D), lambda i, j: (i, 0)),
            pl.BlockSpec((1, D), lambda i, j: (0, 0)),
            pl.BlockSpec((1, n_mod, D), lambda i, j: (i // per_b, 0, 0)),
        ] + w_specs + extra_specs,
        out_specs=out_specs,
        out_shape=out_shapes,
        scratch_shapes=[pltpu.VMEM((tm, D), BF16), *scratch],
        compiler_params=_params(("parallel", "arbitrary")),
        name=name,
    )(x, gain, mod, *([w] * len(col_offsets)), *extras)


def _hgrn_epilogue(accs, extra_refs, out_refs, scratch_refs):
    q, ff, v, g = accs
    log_lb, log_1m_lb, one_m_lb = (r[...] for r in extra_refs)
    q_ref, lf_ref, k_ref, v_ref, sg_ref = out_refs
    q_ref[...] = _silu(q)
    lf_ref[...] = jnp.logaddexp(log_lb, log_1m_lb + jax.nn.log_sigmoid(ff))
    k_ref[...] = one_m_lb * jax.nn.sigmoid(-ff)
    v_ref[...] = v.astype(BF16)
    sg_ref[...] = _silu(g).astype(BF16)


def _store_dilated(a, scr_ref, out_ref, dil):
    rows, width = a.shape
    for h in range(width // HEAD_DIM):
        ah = a[:, h * HEAD_DIM:(h + 1) * HEAD_DIM]
        if dil == 1:
            out_ref[0, h, 0] = ah.astype(BF16)
        else:
            scr_ref[h] = ah
            for r in range(dil):
                out_ref[0, h, r] = scr_ref[h, pl.ds(r, rows // dil, stride=dil), :].astype(BF16)


def _q_epilogue(accs, extra_refs, out_refs, scratch_refs):
    for gi, (_, dil) in enumerate(ATTN_GROUPS):
        @pl.when(pl.program_id(1) == gi)
        def _(gi=gi, dil=dil):
            qn = _head_norm(accs[0], extra_refs[0][0])
            _store_dilated(qn, scratch_refs[0], out_refs[gi], dil)


def _kv_epilogue(accs, extra_refs, out_refs, scratch_refs):
    n = len(ATTN_GROUPS)
    for gi, (_, dil) in enumerate(ATTN_GROUPS):
        @pl.when(pl.program_id(1) == gi)
        def _(gi=gi, dil=dil):
            kn = _head_norm(accs[0], extra_refs[0][0])
            _store_dilated(kn, scratch_refs[0], out_refs[gi], dil)
            _store_dilated(accs[1], scratch_refs[0], out_refs[n + gi], dil)


def _dilated_out(batch, seq, tm, n_heads):
    per_b = seq // tm
    specs, shapes = [], []
    for _, dil in ATTN_GROUPS:
        specs.append(pl.BlockSpec((1, n_heads, dil, tm // dil, HEAD_DIM),
                                  lambda i, j: (i // per_b, 0, 0, i % per_b, 0)))
        shapes.append(jax.ShapeDtypeStruct((batch, n_heads, dil, seq // dil, HEAD_DIM), BF16))
    return specs, shapes


def _hgrn_levels():
    P, out = 2 * HGRN_DIAG, []
    while P <= HGRN_CHUNK:
        out.append(P)
        P *= 2
    return out


def _hgrn_masks():
    t = np.arange(HGRN_CHUNK)
    rows = []
    for P in _hgrn_levels():
        same = (t[:, None] // P) == (t[None, :] // P)
        rows.append(same & ((t[:, None] % P) >= P // 2) & ((t[None, :] % P) < P // 2))
    return jnp.asarray(np.stack(rows), F32)


def _hgrn_chunk(qs, lf, k, v, S, tri, delta, same_diag_block, tok, mask_ref):
    C, c = HGRN_CHUNK, HGRN_DIAG
    lf_hi = lf.astype(BF16)
    lf_lo = (lf - lf_hi.astype(F32)).astype(BF16)
    G = _dot(tri, lf_hi) + _dot(tri, lf_lo)
    GT, qT, kT = G.T, qs.T, k.T

    A = jnp.zeros((C, C), F32)
    for d in range(c):
        if d == 0:
            w = qT * kT
        else:
            Gr = pltpu.roll(GT, C - d, axis=1)
            qr = pltpu.roll(qT, C - d, axis=1)
            w = qr * kT * jnp.exp(Gr - GT)
        diag = jnp.sum(w, axis=0, keepdims=True)
        A = jnp.where(delta == d, jnp.broadcast_to(diag, (C, C)), A)
    A = jnp.where(same_diag_block, A, 0.0)

    for li, P in enumerate(_hgrn_levels()):
        half = P // 2
        Ge = jnp.concatenate(
            [jnp.broadcast_to(G[b * P + half - 1:b * P + half, :], (P, HEAD_DIM))
             for b in range(C // P)], axis=0)
        second = (tok % P) >= half
        E = jnp.exp(jnp.where(second, 1.0, -1.0) * (G - Ge))
        Z = (jnp.where(second, qs, k) * E).astype(BF16)
        A = A + mask_ref[li] * _dot_nt(Z, Z)

    o = _dot(A.astype(BF16), v) + _dot((qs * jnp.exp(G)).astype(BF16), S.astype(BF16))
    g_last = GT[:, C - 1:C]
    k_dec = (kT * jnp.exp(g_last - GT)).astype(BF16)
    S_new = S * jnp.exp(g_last) + _dot(k_dec, v)
    return o, S_new


def _hgrn_kernel(q_ref, lf_ref, k_ref, v_ref, sg_ref, gain_ref, mask_ref, o_ref, st_ref,
                 *, n_chunks, n_heads):
    C, c = HGRN_CHUNK, HGRN_DIAG
    row = lax.broadcasted_iota(jnp.int32, (C, C), 0)
    col = lax.broadcasted_iota(jnp.int32, (C, C), 1)
    tri = (col <= row).astype(BF16)
    delta = row - col
    same_diag_block = (row // c) == (col // c)
    tok = lax.broadcasted_iota(jnp.int32, (C, 1), 0)
    gain = gain_ref[...]
    st_ref[...] = jnp.zeros_like(st_ref)

    def chunk(ci, carry):
        rows = pl.ds(pl.multiple_of(ci * C, C), C)
        for h in range(n_heads):
            sl = slice(h * HEAD_DIM, (h + 1) * HEAD_DIM)
            o, S_new = _hgrn_chunk(q_ref[rows, sl], lf_ref[rows, sl], k_ref[rows, sl],
                                   v_ref[rows, sl], st_ref[h], tri, delta, same_diag_block,
                                   tok, mask_ref)
            st_ref[h] = S_new
            ms = jnp.mean(o * o, axis=-1, keepdims=True)
            on = o * lax.rsqrt(ms + EPS) * gain
            o_ref[rows, sl] = (on * sg_ref[rows, sl].astype(F32)).astype(BF16)
        return carry

    lax.fori_loop(0, n_chunks, chunk, 0)


def _hgrn(q, lf, k, v, sg, out_gain, seq):
    M, D = q.shape
    hb = HGRN_HEADS_PER_STEP
    masks = _hgrn_masks()
    blk = lambda: pl.BlockSpec((seq, hb * HEAD_DIM), lambda b, h: (b, h))
    return pl.pallas_call(
        functools.partial(_hgrn_kernel, n_chunks=seq // HGRN_CHUNK, n_heads=hb),
        grid=(M // seq, D // (hb * HEAD_DIM)),
        in_specs=[blk(), blk(), blk(), blk(), blk(),
                  pl.BlockSpec((1, HEAD_DIM), lambda b, h: (0, 0)),
                  pl.BlockSpec(masks.shape, lambda b, h: (0, 0, 0))],
        out_specs=blk(),
        out_shape=jax.ShapeDtypeStruct((M, D), BF16),
        scratch_shapes=[pltpu.VMEM((hb, HEAD_DIM, HEAD_DIM), F32)],
        compiler_params=_params(("parallel", "parallel")),
        name="hgrn",
    )(q, lf, k, v, sg, out_gain, masks)


def _oproj_kernel(a_ref, w_ref, x_ref, mod_ref, o_ref, *, gate_row):
    gate = mod_ref[0, gate_row:gate_row + 1, :]
    o_ref[...] = x_ref[...] + (1 + gate) * _dot(a_ref[...], w_ref[...])


def _oproj(a, w, x, mod, gate_row, seq):
    M, K = a.shape
    D = w.shape[1]
    tm = _pick(seq, 1024, 8)
    tn = _pick(D, 1024)
    per_b = seq // tm
    return pl.pallas_call(
        functools.partial(_oproj_kernel, gate_row=gate_row),
        grid=(M // tm, D // tn),
        in_specs=[
            pl.BlockSpec((tm, K), lambda i, j: (i, 0)),
            pl.BlockSpec((K, tn), lambda i, j: (0, j)),
            pl.BlockSpec((tm, tn), lambda i, j: (i, j)),
            pl.BlockSpec((1, 9, tn), lambda i, j: (i // per_b, 0, j)),
        ],
        out_specs=pl.BlockSpec((tm, tn), lambda i, j: (i, j)),
        out_shape=jax.ShapeDtypeStruct((M, D), F32),
        compiler_params=_params(("parallel", "arbitrary")),
        name="oproj",
    )(a, w, x, mod)


def _attn_block(q, kc, vc, kp, vp, mask_c, mask_p):
    scale = HEAD_DIM ** -0.5
    s_c = jnp.where(mask_c, _dot_nt(q, kc) * scale, MASK_VALUE)
    m = jnp.max(s_c, axis=-1, keepdims=True)
    if kp is not None:
        s_p = jnp.where(mask_p, _dot_nt(q, kp) * scale, MASK_VALUE)
        m = jnp.maximum(m, jnp.max(s_p, axis=-1, keepdims=True))
    p_c = jnp.where(mask_c, jnp.exp(s_c - m), 0.0)
    l = jnp.sum(p_c, axis=-1, keepdims=True)
    o = _dot(p_c.astype(BF16), vc)
    if kp is not None:
        p_p = jnp.where(mask_p, jnp.exp(s_p - m), 0.0)
        l = l + jnp.sum(p_p, axis=-1, keepdims=True)
        o = o + _dot(p_p.astype(BF16), vp)
    return o, m, l


def _attn_kernel(*refs, seq):
    n_g = len(ATTN_GROUPS)
    qkv = refs[:3 * n_g]
    out_ref = refs[3 * n_g]
    o_scr, m_scr, l_scr = refs[3 * n_g + 1:]
    T = ATTN_BLOCK
    row = lax.broadcasted_iota(jnp.int32, (T, T), 0)
    col = lax.broadcasted_iota(jnp.int32, (T, T), 1)
    mask_c = col <= row
    mask_p = col >= row
    for gi, (_, dil) in enumerate(ATTN_GROUPS):
        q_ref, k_ref, v_ref = qkv[3 * gi:3 * gi + 3]
        for r in range(dil):
            for n in range(seq // dil // T):
                cur = slice(n * T, (n + 1) * T)
                prev = slice((n - 1) * T, n * T)
                kp = k_ref[0, 0, r, prev, :] if n > 0 else None
                vp = v_ref[0, 0, r, prev, :] if n > 0 else None
                o, m, l = _attn_block(q_ref[0, 0, r, cur, :], k_ref[0, 0, r, cur, :],
                                      v_ref[0, 0, r, cur, :], kp, vp, mask_c, mask_p)
                idx = pl.ds(n * T * dil + r, T, stride=dil) if dil > 1 else pl.ds(n * T, T)
                o_scr[gi, idx, :] = o
                m_scr[gi, idx, :] = jnp.broadcast_to(m, (T, HEAD_DIM))
                l_scr[gi, idx, :] = jnp.broadcast_to(l, (T, HEAD_DIM))
    for t in range(seq // T):
        rows = pl.ds(t * T, T)
        ms = [m_scr[gi, rows, :] for gi in range(n_g)]
        mx = functools.reduce(jnp.maximum, ms)
        ws = [jnp.exp(m - mx) for m in ms]
        num = sum(w * o_scr[gi, rows, :] for gi, w in enumerate(ws))
        den = sum(w * l_scr[gi, rows, :] for gi, w in enumerate(ws))
        out_ref[rows, :] = (num / den).astype(BF16)


def _attention(qs, ks, vs, batch, seq, n_heads):
    n_g = len(ATTN_GROUPS)
    in_specs, args = [], []
    for gi, (_, dil) in enumerate(ATTN_GROUPS):
        spec = pl.BlockSpec((1, 1, dil, seq // dil, HEAD_DIM), lambda b, h: (b, h, 0, 0, 0))
        in_specs += [spec, spec, spec]
        args += [qs[gi], ks[gi], vs[gi]]
    return pl.pallas_call(
        functools.partial(_attn_kernel, seq=seq),
        grid=(batch, n_heads),
        in_specs=in_specs,
        out_specs=pl.BlockSpec((seq, HEAD_DIM), lambda b, h: (b, h)),
        out_shape=jax.ShapeDtypeStruct((batch * seq, n_heads * HEAD_DIM), BF16),
        scratch_shapes=[pltpu.VMEM((n_g, seq, HEAD_DIM), F32)] * 3,
        compiler_params=_params(("parallel", "parallel")),
        name="attn",
    )(*args)


def kernel(x, c, norm_g, w_ada, b_ada, w_ffn_in, w_ffn_out, hgrn_w_in, hgrn_w_out, hgrn_lb_logits, hgrn_out_gain, kv_norm_g, kv_w_ada, kv_b_ada, w_kv, k_gain, attn_w_q, attn_q_gain, attn_w_o):
    B, S, D = x.shape
    depth = w_ada.shape[0]
    n_a = hgrn_w_in.shape[0]
    n_groups = k_gain.shape[0]
    width = attn_w_o.shape[1]
    n_heads = width // HEAD_DIM
    GW = n_groups * width
    M = B * S
    assert n_groups == len(ATTN_GROUPS)
    assert all(win // dil == ATTN_BLOCK and S % (dil * ATTN_BLOCK) == 0 for win, dil in ATTN_GROUPS)

    mods = _ada(c, w_ada, b_ada).reshape(depth, B, 9, D)
    kv_mod = _ada(c, kv_w_ada[None], kv_b_ada[None]).reshape(B, 2, D)

    p = jax.nn.softmax(hgrn_lb_logits.astype(F32), axis=0)
    lb = jnp.cumsum(p, axis=0) - p[0]
    log_lb = jnp.log(jnp.maximum(lb, LB_FLOOR))
    log_1m_lb = jnp.log1p(-lb)
    one_m_lb = 1 - lb

    w_ffn_in_b = w_ffn_in.astype(BF16)
    w_ffn_out_b = w_ffn_out.astype(BF16)
    hgrn_w_in_b = hgrn_w_in.astype(BF16)
    hgrn_w_out_b = hgrn_w_out.astype(BF16)
    w_kv_b = w_kv.astype(BF16)
    attn_w_q_b = attn_w_q.astype(BF16)
    attn_w_o_b = attn_w_o.astype(BF16)

    gain_spec = pl.BlockSpec((1, 1, HEAD_DIM), lambda i, j: (j, 0, 0))
    tile = lambda tn: pl.BlockSpec((_pick(S, 512, 8), tn), lambda i, j: (i, j))

    xf = x.reshape(M, D)
    ks = vs = None
    for l in range(depth):
        mod = mods[l]
        xf = _ffn(xf, norm_g[l, 0][None], mod, w_ffn_in_b[l, 0], w_ffn_out_b[l, 0], 0, S)
        if l < n_a:
            tn = _pick(D, 512)
            vec = pl.BlockSpec((1, tn), lambda i, j: (0, j))
            q, lf, k, v, sg = _proj(
                xf, norm_g[l, 1][None], mod, hgrn_w_in_b[l], [0, D, 2 * D, 3 * D], D, tn,
                _pick(S, 512, 8), S, 3, 4,
                [log_lb[l][None], log_1m_lb[l][None], one_m_lb[l][None]], [vec, vec, vec],
                [tile(tn)] * 5,
                [jax.ShapeDtypeStruct((M, D), dt) for dt in (F32, F32, F32, BF16, BF16)],
                _hgrn_epilogue, "hgrn_proj")
            a = _hgrn(q, lf, k, v, sg, hgrn_out_gain[l][None], S)
            xf = _oproj(a, hgrn_w_out_b[l], xf, mod, 5, S)
        else:
            jl = l - n_a
            tm = _pick(S, 512, 8)
            specs, shapes = _dilated_out(B, S, tm, n_heads)
            qs = _proj(
                xf, norm_g[l, 1][None], mod, attn_w_q_b[jl], [0], GW, width, tm, S, 3, 4,
                [attn_q_gain[jl][:, None, :]], [gain_spec], specs, shapes, _q_epilogue,
                "q_proj", scratch=[pltpu.VMEM((n_heads, tm, HEAD_DIM), F32)])
            a = _attention(qs, ks, vs, B, S, n_heads)
            xf = _oproj(a, attn_w_o_b[jl], xf, mod, 5, S)
        xf = _ffn(xf, norm_g[l, 2][None], mod, w_ffn_in_b[l, 1], w_ffn_out_b[l, 1], 2, S)
        if l == n_a - 1:
            tm = _pick(S, 512, 8)
            specs, shapes = _dilated_out(B, S, tm, n_heads)
            kv = _proj(
                xf, kv_norm_g[None], kv_mod, w_kv_b, [0, GW], GW, width, tm, S, 0, 1,
                [k_gain[:, None, :]], [gain_spec], specs * 2, shapes * 2, _kv_epilogue,
                "kv_proj", scratch=[pltpu.VMEM((n_heads, tm, HEAD_DIM), F32)])
            ks, vs = kv[:n_groups], kv[n_groups:]
    return xf.reshape(B, S, D)
```

```python
import functools

import numpy as np
import jax
import jax.numpy as jnp
from jax import lax
from jax.experimental import pallas as pl
from jax.experimental.pallas import tpu as pltpu

F32 = jnp.float32
BF16 = jnp.bfloat16

EPS = 1e-6
MASK_VALUE = -1e30
LB_FLOOR = 1e-30
LOG2_E = 1.4426950408889634
HEAD_DIM = 128
ATTN_GROUPS = ((128, 1), (512, 4), (2048, 16))
ATTN_BLOCK = 128
HGRN_CHUNK = 128
HGRN_DIAG = 2
HGRN_HEADS_PER_STEP = 4

VMEM_LIMIT_BYTES = 56 * 1024 * 1024


def _pick(n, target, mult=128):
    if n <= target:
        return n
    d = (target // mult) * mult
    while d >= mult:
        if n % d == 0:
            return d
        d -= mult
    raise ValueError(f"no tile for {n}")


def _params(sem):
    return pltpu.CompilerParams(dimension_semantics=sem, vmem_limit_bytes=VMEM_LIMIT_BYTES)


def _silu(x):
    return x * jax.nn.sigmoid(x)


def _dot(a, b):
    return jnp.dot(a, b, preferred_element_type=F32)


def _dot_nt(a, b):
    return lax.dot_general(a, b, (((1,), (1,)), ((), ())), preferred_element_type=F32)


NORM_ROWS = 32


def _norm_mod_into(hn_ref, x_ref, gain, shift, scale):
    gs = gain * (1 + scale)
    for r in range(0, x_ref.shape[0], NORM_ROWS):
        x = x_ref[r:r + NORM_ROWS, :]
        ms = jnp.mean(x * x, axis=-1, keepdims=True)
        hn_ref[r:r + NORM_ROWS, :] = (x * lax.rsqrt(ms + EPS) * gs + shift).astype(BF16)


def _head_norm(a, gain):
    outs = []
    for h in range(a.shape[1] // HEAD_DIM):
        ah = a[:, h * HEAD_DIM:(h + 1) * HEAD_DIM]
        ms = jnp.mean(ah * ah, axis=-1, keepdims=True)
        outs.append(ah * lax.rsqrt(ms + EPS) * gain)
    return outs[0] if len(outs) == 1 else jnp.concatenate(outs, axis=1)


def _ada_kernel(c_ref, w_ref, b_ref, o_ref):
    sc = _silu(c_ref[...]).astype(BF16)
    o_ref[0] = _dot(sc, w_ref[0].astype(BF16)) + b_ref[0]


def _ada(c, w, b):
    L, D, N = w.shape
    B = c.shape[0]
    tn = _pick(N, 1024)
    return pl.pallas_call(
        _ada_kernel,
        grid=(L, N // tn),
        in_specs=[
            pl.BlockSpec((B, D), lambda l, j: (0, 0)),
            pl.BlockSpec((1, D, tn), lambda l, j: (l, 0, j)),
            pl.BlockSpec((1, 1, tn), lambda l, j: (l, 0, j)),
        ],
        out_specs=pl.BlockSpec((1, B, tn), lambda l, j: (l, 0, j)),
        out_shape=jax.ShapeDtypeStruct((L, B, N), F32),
        compiler_params=_params(("parallel", "parallel")),
        name="ada",
    )(c, w, b.reshape(L, 1, N))


def _ffn_kernel(x_ref, g_ref, mod_ref, wg_ref, wu_ref, wo_ref, o_ref, hn_ref, *, sub, nj):
    j = pl.program_id(1)

    @pl.when(j == 0)
    def _():
        _norm_mod_into(hn_ref, x_ref, g_ref[...], mod_ref[0, 3 * sub:3 * sub + 1, :],
                       mod_ref[0, 3 * sub + 1:3 * sub + 2, :])
        o_ref[...] = jnp.zeros_like(o_ref)

    hn = hn_ref[...]
    hg = _dot(hn, wg_ref[...])
    hu = _dot(hn, wu_ref[...])
    act = (_silu(hg) * hu).astype(BF16)
    o_ref[...] += _dot(act, wo_ref[...])

    @pl.when(j == nj - 1)
    def _():
        gate = mod_ref[0, 3 * sub + 2:3 * sub + 3, :]
        o_ref[...] = x_ref[...] + (0.5 * (1 + gate)) * o_ref[...]


def _ffn(x, gain, mod, w_in, w_out, sub, seq):
    M, D = x.shape
    F = w_out.shape[0]
    tm = _pick(seq, 512, 8)
    tf = _pick(F, 512)
    nj = F // tf
    per_b = seq // tm
    return pl.pallas_call(
        functools.partial(_ffn_kernel, sub=sub, nj=nj),
        grid=(M // tm, nj),
        in_specs=[
            pl.BlockSpec((tm, D), lambda i, j: (i, 0)),
            pl.BlockSpec((1, D), lambda i, j: (0, 0)),
            pl.BlockSpec((1, 9, D), lambda i, j: (i // per_b, 0, 0)),
            pl.BlockSpec((D, tf), lambda i, j: (0, j)),
            pl.BlockSpec((D, tf), lambda i, j: (0, nj + j)),
            pl.BlockSpec((tf, D), lambda i, j: (j, 0)),
        ],
        out_specs=pl.BlockSpec((tm, D), lambda i, j: (i, 0)),
        out_shape=jax.ShapeDtypeStruct((M, D), F32),
        scratch_shapes=[pltpu.VMEM((tm, D), BF16)],
        compiler_params=_params(("parallel", "arbitrary")),
        name="ffn",
    )(x, gain, mod, w_in, w_in, w_out)


def _proj_kernel(*refs, n_w, n_extra, n_out, shift_row, scale_row, epilogue):
    x_ref, g_ref, mod_ref = refs[:3]
    w_refs = refs[3:3 + n_w]
    extra_refs = refs[3 + n_w:3 + n_w + n_extra]
    out_refs = refs[3 + n_w + n_extra:3 + n_w + n_extra + n_out]
    hn_ref = refs[3 + n_w + n_extra + n_out]
    scratch_refs = refs[4 + n_w + n_extra + n_out:]

    @pl.when(pl.program_id(1) == 0)
    def _():
        _norm_mod_into(hn_ref, x_ref, g_ref[...], mod_ref[0, shift_row:shift_row + 1, :],
                       mod_ref[0, scale_row:scale_row + 1, :])

    hn = hn_ref[...]
    accs = [_dot(hn, w[...]) for w in w_refs]
    epilogue(accs, extra_refs, out_refs, scratch_refs)


def _proj(x, gain, mod, w, col_offsets, n_cols, tn, tm, seq, shift_row, scale_row,
          extras, extra_specs, out_specs, out_shapes, epilogue, name, scratch=()):
    M, D = x.shape
    per_b = seq // tm
    n_mod = mod.shape[1]
    w_specs = [
        pl.BlockSpec((D, tn), functools.partial(lambda i, j, o: (0, o + j), o=off // tn))
        for off in col_offsets
    ]
    return pl.pallas_call(
        functools.partial(_proj_kernel, n_w=len(col_offsets), n_extra=len(extras),
                          n_out=len(out_specs), shift_row=shift_row, scale_row=scale_row,
                          epilogue=epilogue),
        grid=(M // tm, n_cols // tn),
        in_specs=[
            pl.BlockSpec((tm, D), lambda i, j: (i, 0)),
            pl.BlockSpec((1, D), lambda i, j: (0, 0)),
            pl.BlockSpec((1, n_mod, D), lambda i, j: (i // per_b, 0, 0)),
        ] + w_specs + extra_specs,
        out_specs=out_specs,
        out_shape=out_shapes,
        scratch_shapes=[pltpu.VMEM((tm, D), BF16), *scratch],
        compiler_params=_params(("parallel", "arbitrary")),
        name=name,
    )(x, gain, mod, *([w] * len(col_offsets)), *extras)


def _hgrn_epilogue(accs, extra_refs, out_refs, scratch_refs):
    q, ff, v, g = accs
    lb_floor, one_m_lb = (r[...] for r in extra_refs)
    q_ref, lf_ref, k_ref, v_ref, sg_ref = out_refs
    q_ref[...] = _silu(q)
    e = jnp.exp(-jnp.abs(ff))
    r = 1.0 / (1.0 + e)
    pos = ff >= 0
    lf_ref[...] = jnp.log(lb_floor + one_m_lb * jnp.where(pos, r, e * r)) * LOG2_E
    k_ref[...] = one_m_lb * jnp.where(pos, e * r, r)
    v_ref[...] = v.astype(BF16)
    sg_ref[...] = _silu(g).astype(BF16)


def _store_dilated(a, scr_ref, out_ref, dil):
    rows, width = a.shape
    for h in range(width // HEAD_DIM):
        ah = a[:, h * HEAD_DIM:(h + 1) * HEAD_DIM]
        if dil == 1:
            out_ref[0, h, 0] = ah.astype(BF16)
        else:
            scr_ref[h] = ah
            for r in range(dil):
                out_ref[0, h, r] = scr_ref[h, pl.ds(r, rows // dil, stride=dil), :].astype(BF16)


def _q_epilogue(accs, extra_refs, out_refs, scratch_refs):
    for gi, (_, dil) in enumerate(ATTN_GROUPS):
        @pl.when(pl.program_id(1) == gi)
        def _(gi=gi, dil=dil):
            qn = _head_norm(accs[0], extra_refs[0][0])
            _store_dilated(qn, scratch_refs[0], out_refs[gi], dil)


def _kv_epilogue(accs, extra_refs, out_refs, scratch_refs):
    n = len(ATTN_GROUPS)
    for gi, (_, dil) in enumerate(ATTN_GROUPS):
        @pl.when(pl.program_id(1) == gi)
        def _(gi=gi, dil=dil):
            kn = _head_norm(accs[0], extra_refs[0][0])
            _store_dilated(kn, scratch_refs[0], out_refs[gi], dil)
            _store_dilated(accs[1], scratch_refs[0], out_refs[n + gi], dil)


def _dilated_out(batch, seq, tm, n_heads):
    per_b = seq // tm
    specs, shapes = [], []
    for _, dil in ATTN_GROUPS:
        specs.append(pl.BlockSpec((1, n_heads, dil, tm // dil, HEAD_DIM),
                                  lambda i, j: (i // per_b, 0, 0, i % per_b, 0)))
        shapes.append(jax.ShapeDtypeStruct((batch, n_heads, dil, seq // dil, HEAD_DIM), BF16))
    return specs, shapes


def _hgrn_levels():
    P, out = 2 * HGRN_DIAG, []
    while P <= HGRN_CHUNK:
        out.append(P)
        P *= 2
    return out


def _hgrn_masks():
    t = np.arange(HGRN_CHUNK)
    rows = []
    for P in _hgrn_levels():
        same = (t[:, None] // P) == (t[None, :] // P)
        rows.append(same & ((t[:, None] % P) >= P // 2) & ((t[None, :] % P) < P // 2))
    return jnp.asarray(np.stack(rows), F32)


def _dot_tn(a, b):
    return lax.dot_general(a, b, (((0,), (0,)), ((), ())), preferred_element_type=F32)


def _hgrn_chunk(qs, lf, k, v, ST, tri, ones, delta, same_diag_block, tok, mask_ref):
    C, c = HGRN_CHUNK, HGRN_DIAG
    lf_hi = lf.astype(BF16)
    lf_lo = (lf - lf_hi.astype(F32)).astype(BF16)
    G = _dot(tri, lf_hi) + _dot(tri, lf_lo)

    W = [(qs * k).astype(BF16)]
    for d in range(1, c):
        ks = pltpu.roll(k, d, axis=0)
        Gs = pltpu.roll(G, d, axis=0)
        W.append((qs * ks * jnp.exp2(G - Gs)).astype(BF16))
    R = _dot(jnp.concatenate(W, axis=0), ones)
    A = jnp.zeros((C, C), F32)
    for d in range(c):
        A = jnp.where(delta == d, R[d * C:(d + 1) * C], A)
    A = jnp.where(same_diag_block, A, 0.0)

    for li, P in enumerate(_hgrn_levels()):
        half = P // 2
        if half % 8 == 0:
            src, expo = [], []
            for b in range(C // P):
                lo, mid, hi = b * P, b * P + half, (b + 1) * P
                g_mid = G[mid - 1:mid, :]
                src += [k[lo:mid], qs[mid:hi]]
                expo += [g_mid - G[lo:mid], G[mid:hi] - g_mid]
            Z = jnp.concatenate(src, axis=0) * jnp.exp2(jnp.concatenate(expo, axis=0))
        else:
            Ge = jnp.concatenate(
                [jnp.broadcast_to(G[b * P + half - 1:b * P + half, :], (P, HEAD_DIM))
                 for b in range(C // P)], axis=0)
            second = (tok % P) >= half
            Z = jnp.where(second, qs, k) * jnp.exp2(jnp.where(second, 1.0, -1.0) * (G - Ge))
        Z = Z.astype(BF16)
        A = A + mask_ref[li] * _dot_nt(Z, Z)

    o = _dot(A.astype(BF16), v) + _dot_nt((qs * jnp.exp2(G)).astype(BF16), ST.astype(BF16))
    g_last = G[C - 1:C, :]
    k_dec = (k * jnp.exp2(g_last - G)).astype(BF16)
    ST_new = ST * jnp.exp2(g_last) + _dot_tn(v, k_dec)
    return o, ST_new


def _hgrn_kernel(q_ref, lf_ref, k_ref, v_ref, sg_ref, gain_ref, mask_ref, o_ref, st_ref,
                 *, n_chunks, n_heads):
    C, c = HGRN_CHUNK, HGRN_DIAG
    row = lax.broadcasted_iota(jnp.int32, (C, C), 0)
    col = lax.broadcasted_iota(jnp.int32, (C, C), 1)
    tri = (col <= row).astype(BF16)
    ones = jnp.ones((HEAD_DIM, C), BF16)
    delta = row - col
    same_diag_block = (row // c) == (col // c)
    tok = lax.broadcasted_iota(jnp.int32, (C, 1), 0)
    gain = gain_ref[...]
    st_ref[...] = jnp.zeros_like(st_ref)

    def chunk(ci, carry):
        rows = pl.ds(pl.multiple_of(ci * C, C), C)
        for h in range(n_heads):
            sl = slice(h * HEAD_DIM, (h + 1) * HEAD_DIM)
            o, S_new = _hgrn_chunk(q_ref[rows, sl], lf_ref[rows, sl], k_ref[rows, sl],
                                   v_ref[rows, sl], st_ref[h], tri, ones, delta,
                                   same_diag_block, tok, mask_ref)
            st_ref[h] = S_new
            ms = jnp.mean(o * o, axis=-1, keepdims=True)
            on = o * lax.rsqrt(ms + EPS) * gain
            o_ref[rows, sl] = (on * sg_ref[rows, sl].astype(F32)).astype(BF16)
        return carry

    lax.fori_loop(0, n_chunks, chunk, 0, unroll=4)


def _hgrn(q, lf, k, v, sg, out_gain, seq):
    M, D = q.shape
    hb = min(HGRN_HEADS_PER_STEP, D // HEAD_DIM)
    masks = _hgrn_masks()
    blk = lambda: pl.BlockSpec((seq, hb * HEAD_DIM), lambda b, h: (b, h))
    return pl.pallas_call(
        functools.partial(_hgrn_kernel, n_chunks=seq // HGRN_CHUNK, n_heads=hb),
        grid=(M // seq, D // (hb * HEAD_DIM)),
        in_specs=[blk(), blk(), blk(), blk(), blk(),
                  pl.BlockSpec((1, HEAD_DIM), lambda b, h: (0, 0)),
                  pl.BlockSpec(masks.shape, lambda b, h: (0, 0, 0))],
        out_specs=blk(),
        out_shape=jax.ShapeDtypeStruct((M, D), BF16),
        scratch_shapes=[pltpu.VMEM((hb, HEAD_DIM, HEAD_DIM), F32)],
        compiler_params=_params(("parallel", "parallel")),
        name="hgrn",
    )(q, lf, k, v, sg, out_gain, masks)


def _oproj_kernel(a_ref, w_ref, x_ref, mod_ref, o_ref, *, gate_row):
    gate = mod_ref[0, gate_row:gate_row + 1, :]
    o_ref[...] = x_ref[...] + (1 + gate) * _dot(a_ref[...], w_ref[...])


def _oproj(a, w, x, mod, gate_row, seq):
    M, K = a.shape
    D = w.shape[1]
    tm = _pick(seq, 1024, 8)
    tn = _pick(D, 1024)
    per_b = seq // tm
    return pl.pallas_call(
        functools.partial(_oproj_kernel, gate_row=gate_row),
        grid=(M // tm, D // tn),
        in_specs=[
            pl.BlockSpec((tm, K), lambda i, j: (i, 0)),
            pl.BlockSpec((K, tn), lambda i, j: (0, j)),
            pl.BlockSpec((tm, tn), lambda i, j: (i, j)),
            pl.BlockSpec((1, 9, tn), lambda i, j: (i // per_b, 0, j)),
        ],
        out_specs=pl.BlockSpec((tm, tn), lambda i, j: (i, j)),
        out_shape=jax.ShapeDtypeStruct((M, D), F32),
        compiler_params=_params(("parallel", "arbitrary")),
        name="oproj",
    )(a, w, x, mod)


def _attn_block(q, kc, vc, kp, vp, mask_c, mask_p):
    scale = HEAD_DIM ** -0.5
    s_c = jnp.where(mask_c, _dot_nt(q, kc) * scale, MASK_VALUE)
    m = jnp.max(s_c, axis=-1, keepdims=True)
    if kp is not None:
        s_p = jnp.where(mask_p, _dot_nt(q, kp) * scale, MASK_VALUE)
        m = jnp.maximum(m, jnp.max(s_p, axis=-1, keepdims=True))
    p_c = jnp.where(mask_c, jnp.exp(s_c - m), 0.0)
    l = jnp.sum(p_c, axis=-1, keepdims=True)
    o = _dot(p_c.astype(BF16), vc)
    if kp is not None:
        p_p = jnp.where(mask_p, jnp.exp(s_p - m), 0.0)
        l = l + jnp.sum(p_p, axis=-1, keepdims=True)
        o = o + _dot(p_p.astype(BF16), vp)
    return o, m, l


def _attn_kernel(*refs, seq):
    n_g = len(ATTN_GROUPS)
    qkv = refs[:3 * n_g]
    out_ref = refs[3 * n_g]
    o_scr, m_scr, l_scr = refs[3 * n_g + 1:]
    T = ATTN_BLOCK
    row = lax.broadcasted_iota(jnp.int32, (T, T), 0)
    col = lax.broadcasted_iota(jnp.int32, (T, T), 1)
    mask_c = col <= row
    mask_p = col >= row
    for gi, (_, dil) in enumerate(ATTN_GROUPS):
        q_ref, k_ref, v_ref = qkv[3 * gi:3 * gi + 3]
        for r in range(dil):
            for n in range(seq // dil // T):
                cur = slice(n * T, (n + 1) * T)
                prev = slice((n - 1) * T, n * T)
                kp = k_ref[0, 0, r, prev, :] if n > 0 else None
                vp = v_ref[0, 0, r, prev, :] if n > 0 else None
                o, m, l = _attn_block(q_ref[0, 0, r, cur, :], k_ref[0, 0, r, cur, :],
                                      v_ref[0, 0, r, cur, :], kp, vp, mask_c, mask_p)
                idx = pl.ds(n * T * dil + r, T, stride=dil) if dil > 1 else pl.ds(n * T, T)
                o_scr[gi, idx, :] = o
                m_scr[gi, idx, :] = jnp.broadcast_to(m, (T, HEAD_DIM))
                l_scr[gi, idx, :] = jnp.broadcast_to(l, (T, HEAD_DIM))
    for t in range(seq // T):
        rows = pl.ds(t * T, T)
        ms = [m_scr[gi, rows, :] for gi in range(n_g)]
        mx = functools.reduce(jnp.maximum, ms)
        ws = [jnp.exp(m - mx) for m in ms]
        num = sum(w * o_scr[gi, rows, :] for gi, w in enumerate(ws))
        den = sum(w * l_scr[gi, rows, :] for gi, w in enumerate(ws))
        out_ref[rows, :] = (num / den).astype(BF16)


def _attention(qs, ks, vs, batch, seq, n_heads):
    n_g = len(ATTN_GROUPS)
    in_specs, args = [], []
    for gi, (_, dil) in enumerate(ATTN_GROUPS):
        spec = pl.BlockSpec((1, 1, dil, seq // dil, HEAD_DIM), lambda b, h: (b, h, 0, 0, 0))
        in_specs += [spec, spec, spec]
        args += [qs[gi], ks[gi], vs[gi]]
    return pl.pallas_call(
        functools.partial(_attn_kernel, seq=seq),
        grid=(batch, n_heads),
        in_specs=in_specs,
        out_specs=pl.BlockSpec((seq, HEAD_DIM), lambda b, h: (b, h)),
        out_shape=jax.ShapeDtypeStruct((batch * seq, n_heads * HEAD_DIM), BF16),
        scratch_shapes=[pltpu.VMEM((n_g, seq, HEAD_DIM), F32)] * 3,
        compiler_params=_params(("parallel", "parallel")),
        name="attn",
    )(*args)


def kernel(x, c, norm_g, w_ada, b_ada, w_ffn_in, w_ffn_out, hgrn_w_in, hgrn_w_out, hgrn_lb_logits, hgrn_out_gain, kv_norm_g, kv_w_ada, kv_b_ada, w_kv, k_gain, attn_w_q, attn_q_gain, attn_w_o):
    B, S, D = x.shape
    depth = w_ada.shape[0]
    n_a = hgrn_w_in.shape[0]
    n_groups = k_gain.shape[0]
    width = attn_w_o.shape[1]
    n_heads = width // HEAD_DIM
    GW = n_groups * width
    M = B * S
    assert n_groups == len(ATTN_GROUPS)
    assert all(win // dil == ATTN_BLOCK and S % (dil * ATTN_BLOCK) == 0 for win, dil in ATTN_GROUPS)

    mods = _ada(c, w_ada, b_ada).reshape(depth, B, 9, D)
    kv_mod = _ada(c, kv_w_ada[None], kv_b_ada[None]).reshape(B, 2, D)

    p = jax.nn.softmax(hgrn_lb_logits.astype(F32), axis=0)
    lb = jnp.cumsum(p, axis=0) - p[0]
    lb_floor = jnp.maximum(lb, LB_FLOOR)
    one_m_lb = 1 - lb
    bf = lambda w: w.astype(BF16)

    gain_spec = pl.BlockSpec((1, 1, HEAD_DIM), lambda i, j: (j, 0, 0))
    tile = lambda tn: pl.BlockSpec((_pick(S, 512, 8), tn), lambda i, j: (i, j))

    xf = x.reshape(M, D)
    ks = vs = None
    for l in range(depth):
        mod = mods[l]
        xf = _ffn(xf, norm_g[l, 0][None], mod, bf(w_ffn_in[l, 0]), bf(w_ffn_out[l, 0]), 0, S)
        if l < n_a:
            tn = _pick(D, 512)
            vec = pl.BlockSpec((1, tn), lambda i, j: (0, j))
            q, lf, k, v, sg = _proj(
                xf, norm_g[l, 1][None], mod, bf(hgrn_w_in[l]), [0, D, 2 * D, 3 * D], D, tn,
                _pick(S, 512, 8), S, 3, 4,
                [lb_floor[l][None], one_m_lb[l][None]], [vec, vec],
                [tile(tn)] * 5,
                [jax.ShapeDtypeStruct((M, D), dt) for dt in (F32, F32, F32, BF16, BF16)],
                _hgrn_epilogue, "hgrn_proj")
            a = _hgrn(q, lf, k, v, sg, hgrn_out_gain[l][None], S)
            xf = _oproj(a, bf(hgrn_w_out[l]), xf, mod, 5, S)
        else:
            jl = l - n_a
            tm = _pick(S, 512, 8)
            specs, shapes = _dilated_out(B, S, tm, n_heads)
            qs = _proj(
                xf, norm_g[l, 1][None], mod, bf(attn_w_q[jl]), [0], GW, width, tm, S, 3, 4,
                [attn_q_gain[jl][:, None, :]], [gain_spec], specs, shapes, _q_epilogue,
                "q_proj", scratch=[pltpu.VMEM((n_heads, tm, HEAD_DIM), F32)])
            a = _attention(qs, ks, vs, B, S, n_heads)
            xf = _oproj(a, bf(attn_w_o[jl]), xf, mod, 5, S)
        xf = _ffn(xf, norm_g[l, 2][None], mod, bf(w_ffn_in[l, 1]), bf(w_ffn_out[l, 1]), 2, S)
        if l == n_a - 1:
            tm = _pick(S, 512, 8)
            specs, shapes = _dilated_out(B, S, tm, n_heads)
            kv = _proj(
                xf, kv_norm_g[None], kv_mod, bf(w_kv), [0, GW], GW, width, tm, S, 0, 1,
                [k_gain[:, None, :]], [gain_spec], specs * 2, shapes * 2, _kv_epilogue,
                "kv_proj", scratch=[pltpu.VMEM((n_heads, tm, HEAD_DIM), F32)])
            ks, vs = kv[:n_groups], kv[n_groups:]
    return xf.reshape(B, S, D)
```

```python
import functools

import numpy as np
import jax
import jax.numpy as jnp
from jax import lax
from jax.experimental import pallas as pl
from jax.experimental.pallas import tpu as pltpu

F32 = jnp.float32
BF16 = jnp.bfloat16

EPS = 1e-6
MASK_VALUE = -1e30
LB_FLOOR = 1e-30
LOG2_E = 1.4426950408889634
HEAD_DIM = 128
ATTN_GROUPS = ((128, 1), (512, 4), (2048, 16))
ATTN_BLOCK = 128
HGRN_CHUNK = 128
HGRN_DIAG = 2
HGRN_HEADS_PER_STEP = 4

VMEM_LIMIT_BYTES = 56 * 1024 * 1024


def _pick(n, target, mult=128):
    if n <= target:
        return n
    d = (target // mult) * mult
    while d >= mult:
        if n % d == 0:
            return d
        d -= mult
    raise ValueError(f"no tile for {n}")


def _params(sem):
    return pltpu.CompilerParams(dimension_semantics=sem, vmem_limit_bytes=VMEM_LIMIT_BYTES)


def _silu(x):
    return x * jax.nn.sigmoid(x)


def _dot(a, b):
    return jnp.dot(a, b, preferred_element_type=F32)


def _dot_nt(a, b):
    return lax.dot_general(a, b, (((1,), (1,)), ((), ())), preferred_element_type=F32)


NORM_ROWS = 32


def _norm_mod_into(hn_ref, x_ref, gain, shift, scale):
    gs = gain * (1 + scale)
    for r in range(0, x_ref.shape[0], NORM_ROWS):
        x = x_ref[r:r + NORM_ROWS, :]
        ms = jnp.mean(x * x, axis=-1, keepdims=True)
        hn_ref[r:r + NORM_ROWS, :] = (x * lax.rsqrt(ms + EPS) * gs + shift).astype(BF16)


def _head_norm(a, gain):
    outs = []
    for h in range(a.shape[1] // HEAD_DIM):
        ah = a[:, h * HEAD_DIM:(h + 1) * HEAD_DIM]
        ms = jnp.mean(ah * ah, axis=-1, keepdims=True)
        outs.append(ah * lax.rsqrt(ms + EPS) * gain)
    return outs[0] if len(outs) == 1 else jnp.concatenate(outs, axis=1)


def _ada_kernel(c_ref, w_ref, b_ref, o_ref):
    sc = _silu(c_ref[...]).astype(BF16)
    o_ref[0] = _dot(sc, w_ref[0].astype(BF16)) + b_ref[0]


def _ada(c, w, b):
    L, D, N = w.shape
    B = c.shape[0]
    tn = _pick(N, 1024)
    return pl.pallas_call(
        _ada_kernel,
        grid=(L, N // tn),
        in_specs=[
            pl.BlockSpec((B, D), lambda l, j: (0, 0)),
            pl.BlockSpec((1, D, tn), lambda l, j: (l, 0, j)),
            pl.BlockSpec((1, 1, tn), lambda l, j: (l, 0, j)),
        ],
        out_specs=pl.BlockSpec((1, B, tn), lambda l, j: (l, 0, j)),
        out_shape=jax.ShapeDtypeStruct((L, B, N), F32),
        compiler_params=_params(("parallel", "parallel")),
        name="ada",
    )(c, w, b.reshape(L, 1, N))


def _ffn_kernel(x_ref, g_ref, mod_ref, wg_ref, wu_ref, wo_ref, o_ref, hn_ref, *, sub, nj):
    j = pl.program_id(1)

    @pl.when(j == 0)
    def _():
        _norm_mod_into(hn_ref, x_ref, g_ref[...], mod_ref[0, 3 * sub:3 * sub + 1, :],
                       mod_ref[0, 3 * sub + 1:3 * sub + 2, :])
        o_ref[...] = jnp.zeros_like(o_ref)

    hn = hn_ref[...]
    hg = _dot(hn, wg_ref[...])
    hu = _dot(hn, wu_ref[...])
    act = (_silu(hg) * hu).astype(BF16)
    o_ref[...] += _dot(act, wo_ref[...])

    @pl.when(j == nj - 1)
    def _():
        gate = mod_ref[0, 3 * sub + 2:3 * sub + 3, :]
        o_ref[...] = x_ref[...] + (0.5 * (1 + gate)) * o_ref[...]


def _wspec(lead, block, index_map):
    return pl.BlockSpec((None,) * len(lead) + block, lambda i, j: lead + index_map(i, j))


def _ffn(x, gain, mod, w_in, w_out, lead, sub, seq):
    M, D = x.shape
    F = w_out.shape[-2]
    tm = _pick(seq, 512, 8)
    tf = _pick(F, 512)
    nj = F // tf
    per_b = seq // tm
    return pl.pallas_call(
        functools.partial(_ffn_kernel, sub=sub, nj=nj),
        grid=(M // tm, nj),
        in_specs=[
            pl.BlockSpec((tm, D), lambda i, j: (i, 0)),
            pl.BlockSpec((1, D), lambda i, j: (0, 0)),
            pl.BlockSpec((1, 9, D), lambda i, j: (i // per_b, 0, 0)),
            _wspec(lead, (D, tf), lambda i, j: (0, j)),
            _wspec(lead, (D, tf), lambda i, j: (0, nj + j)),
            _wspec(lead, (tf, D), lambda i, j: (j, 0)),
        ],
        out_specs=pl.BlockSpec((tm, D), lambda i, j: (i, 0)),
        out_shape=jax.ShapeDtypeStruct((M, D), F32),
        scratch_shapes=[pltpu.VMEM((tm, D), BF16)],
        compiler_params=_params(("parallel", "arbitrary")),
        name="ffn",
    )(x, gain, mod, w_in, w_in, w_out)


def _proj_kernel(*refs, n_w, n_extra, n_out, shift_row, scale_row, epilogue):
    x_ref, g_ref, mod_ref = refs[:3]
    w_refs = refs[3:3 + n_w]
    extra_refs = refs[3 + n_w:3 + n_w + n_extra]
    out_refs = refs[3 + n_w + n_extra:3 + n_w + n_extra + n_out]
    hn_ref = refs[3 + n_w + n_extra + n_out]
    scratch_refs = refs[4 + n_w + n_extra + n_out:]

    @pl.when(pl.program_id(1) == 0)
    def _():
        _norm_mod_into(hn_ref, x_ref, g_ref[...], mod_ref[0, shift_row:shift_row + 1, :],
                       mod_ref[0, scale_row:scale_row + 1, :])

    hn = hn_ref[...]
    accs = [_dot(hn, w[...]) for w in w_refs]
    epilogue(accs, extra_refs, out_refs, scratch_refs)


def _proj(x, gain, mod, w, lead, col_offsets, n_cols, tn, tm, seq, shift_row, scale_row,
          extras, extra_specs, out_specs, out_shapes, epilogue, name, scratch=()):
    M, D = x.shape
    per_b = seq // tm
    n_mod = mod.shape[1]
    w_specs = [
        _wspec(lead, (D, tn), functools.partial(lambda i, j, o: (0, o + j), o=off // tn))
        for off in col_offsets
    ]
    return pl.pallas_call(
        functools.partial(_proj_kernel, n_w=len(col_offsets), n_extra=len(extras),
                          n_out=len(out_specs), shift_row=shift_row, scale_row=scale_row,
                          epilogue=epilogue),
        grid=(M // tm, n_cols // tn),
        in_specs=[
            pl.BlockSpec((tm, D), lambda i, j: (i, 0)),
            pl.BlockSpec((1, D), lambda i, j: (0, 0)),
            pl.BlockSpec((1, n_mod, D), lambda i, j: (i // per_b, 0, 0)),
        ] + w_specs + extra_specs,
        out_specs=out_specs,
        out_shape=out_shapes,
        scratch_shapes=[pltpu.VMEM((tm, D), BF16), *scratch],
        compiler_params=_params(("parallel", "arbitrary")),
        name=name,
    )(x, gain, mod, *([w] * len(col_offsets)), *extras)


def _hgrn_epilogue(accs, extra_refs, out_refs, scratch_refs):
    q, ff, v, g = accs
    lb_floor, one_m_lb = (r[...] for r in extra_refs)
    q_ref, lf_ref, k_ref, v_ref, sg_ref = out_refs
    q_ref[...] = _silu(q)
    e = jnp.exp(-jnp.abs(ff))
    r = 1.0 / (1.0 + e)
    pos = ff >= 0
    lf_ref[...] = jnp.log(lb_floor + one_m_lb * jnp.where(pos, r, e * r)) * LOG2_E
    k_ref[...] = one_m_lb * jnp.where(pos, e * r, r)
    v_ref[...] = v.astype(BF16)
    sg_ref[...] = _silu(g).astype(BF16)


def _store_dilated(a, scr_ref, out_ref, dil):
    rows, width = a.shape
    for h in range(width // HEAD_DIM):
        ah = a[:, h * HEAD_DIM:(h + 1) * HEAD_DIM]
        if dil == 1:
            out_ref[0, h, 0] = ah.astype(BF16)
        else:
            scr_ref[h] = ah
            for r in range(dil):
                out_ref[0, h, r] = scr_ref[h, pl.ds(r, rows // dil, stride=dil), :].astype(BF16)


def _q_epilogue(accs, extra_refs, out_refs, scratch_refs):
    for gi, (_, dil) in enumerate(ATTN_GROUPS):
        @pl.when(pl.program_id(1) == gi)
        def _(gi=gi, dil=dil):
            qn = _head_norm(accs[0], extra_refs[0][0])
            _store_dilated(qn, scratch_refs[0], out_refs[gi], dil)


def _kv_epilogue(accs, extra_refs, out_refs, scratch_refs):
    n = len(ATTN_GROUPS)
    for gi, (_, dil) in enumerate(ATTN_GROUPS):
        @pl.when(pl.program_id(1) == gi)
        def _(gi=gi, dil=dil):
            kn = _head_norm(accs[0], extra_refs[0][0])
            _store_dilated(kn, scratch_refs[0], out_refs[gi], dil)
            _store_dilated(accs[1], scratch_refs[0], out_refs[n + gi], dil)


def _dilated_out(batch, seq, tm, n_heads):
    per_b = seq // tm
    specs, shapes = [], []
    for _, dil in ATTN_GROUPS:
        specs.append(pl.BlockSpec((1, n_heads, dil, tm // dil, HEAD_DIM),
                                  lambda i, j: (i // per_b, 0, 0, i % per_b, 0)))
        shapes.append(jax.ShapeDtypeStruct((batch, n_heads, dil, seq // dil, HEAD_DIM), BF16))
    return specs, shapes


def _hgrn_levels():
    P, out = 2 * HGRN_DIAG, []
    while P <= HGRN_CHUNK:
        out.append(P)
        P *= 2
    return out


def _hgrn_masks():
    t = np.arange(HGRN_CHUNK)
    rows = []
    for P in _hgrn_levels():
        same = (t[:, None] // P) == (t[None, :] // P)
        rows.append(same & ((t[:, None] % P) >= P // 2) & ((t[None, :] % P) < P // 2))
    return jnp.asarray(np.stack(rows), F32)


def _dot_tn(a, b):
    return lax.dot_general(a, b, (((0,), (0,)), ((), ())), preferred_element_type=F32)


def _hgrn_chunk(qs, lf, k, v, ST, tri, ones, delta, same_diag_block, tok, mask_ref):
    C, c = HGRN_CHUNK, HGRN_DIAG
    lf_hi = lf.astype(BF16)
    lf_lo = (lf - lf_hi.astype(F32)).astype(BF16)
    G = _dot(tri, lf_hi) + _dot(tri, lf_lo)

    W = [(qs * k).astype(BF16)]
    for d in range(1, c):
        ks = pltpu.roll(k, d, axis=0)
        Gs = pltpu.roll(G, d, axis=0)
        W.append((qs * ks * jnp.exp2(G - Gs)).astype(BF16))
    R = _dot(jnp.concatenate(W, axis=0), ones)
    A = jnp.zeros((C, C), F32)
    for d in range(c):
        A = jnp.where(delta == d, R[d * C:(d + 1) * C], A)
    A = jnp.where(same_diag_block, A, 0.0)

    for li, P in enumerate(_hgrn_levels()):
        half = P // 2
        if half % 8 == 0:
            src, expo = [], []
            for b in range(C // P):
                lo, mid, hi = b * P, b * P + half, (b + 1) * P
                g_mid = G[mid - 1:mid, :]
                src += [k[lo:mid], qs[mid:hi]]
                expo += [g_mid - G[lo:mid], G[mid:hi] - g_mid]
            Z = jnp.concatenate(src, axis=0) * jnp.exp2(jnp.concatenate(expo, axis=0))
        else:
            Ge = jnp.concatenate(
                [jnp.broadcast_to(G[b * P + half - 1:b * P + half, :], (P, HEAD_DIM))
                 for b in range(C // P)], axis=0)
            second = (tok % P) >= half
            Z = jnp.where(second, qs, k) * jnp.exp2(jnp.where(second, 1.0, -1.0) * (G - Ge))
        Z = Z.astype(BF16)
        A = A + mask_ref[li] * _dot_nt(Z, Z)

    o = _dot(A.astype(BF16), v) + _dot_nt((qs * jnp.exp2(G)).astype(BF16), ST.astype(BF16))
    g_last = G[C - 1:C, :]
    k_dec = (k * jnp.exp2(g_last - G)).astype(BF16)
    ST_new = ST * jnp.exp2(g_last) + _dot_tn(v, k_dec)
    return o, ST_new


def _hgrn_kernel(q_ref, lf_ref, k_ref, v_ref, sg_ref, gain_ref, mask_ref, o_ref, st_ref,
                 *, n_chunks, n_heads):
    C, c = HGRN_CHUNK, HGRN_DIAG
    row = lax.broadcasted_iota(jnp.int32, (C, C), 0)
    col = lax.broadcasted_iota(jnp.int32, (C, C), 1)
    tri = (col <= row).astype(BF16)
    ones = jnp.ones((HEAD_DIM, C), BF16)
    delta = row - col
    same_diag_block = (row // c) == (col // c)
    tok = lax.broadcasted_iota(jnp.int32, (C, 1), 0)
    gain = gain_ref[...]
    st_ref[...] = jnp.zeros_like(st_ref)

    def chunk(ci, carry):
        rows = pl.ds(pl.multiple_of(ci * C, C), C)
        for h in range(n_heads):
            sl = slice(h * HEAD_DIM, (h + 1) * HEAD_DIM)
            o, S_new = _hgrn_chunk(q_ref[rows, sl], lf_ref[rows, sl], k_ref[rows, sl],
                                   v_ref[rows, sl], st_ref[h], tri, ones, delta,
                                   same_diag_block, tok, mask_ref)
            st_ref[h] = S_new
            ms = jnp.mean(o * o, axis=-1, keepdims=True)
            on = o * lax.rsqrt(ms + EPS) * gain
            o_ref[rows, sl] = (on * sg_ref[rows, sl].astype(F32)).astype(BF16)
        return carry

    lax.fori_loop(0, n_chunks, chunk, 0, unroll=4)


def _hgrn(q, lf, k, v, sg, out_gain, seq):
    M, D = q.shape
    hb = min(HGRN_HEADS_PER_STEP, D // HEAD_DIM)
    masks = _hgrn_masks()
    blk = lambda: pl.BlockSpec((seq, hb * HEAD_DIM), lambda b, h: (b, h))
    return pl.pallas_call(
        functools.partial(_hgrn_kernel, n_chunks=seq // HGRN_CHUNK, n_heads=hb),
        grid=(M // seq, D // (hb * HEAD_DIM)),
        in_specs=[blk(), blk(), blk(), blk(), blk(),
                  pl.BlockSpec((1, HEAD_DIM), lambda b, h: (0, 0)),
                  pl.BlockSpec(masks.shape, lambda b, h: (0, 0, 0))],
        out_specs=blk(),
        out_shape=jax.ShapeDtypeStruct((M, D), BF16),
        scratch_shapes=[pltpu.VMEM((hb, HEAD_DIM, HEAD_DIM), F32)],
        compiler_params=_params(("parallel", "parallel")),
        name="hgrn",
    )(q, lf, k, v, sg, out_gain, masks)


def _oproj_kernel(a_ref, w_ref, x_ref, mod_ref, o_ref, *, gate_row):
    gate = mod_ref[0, gate_row:gate_row + 1, :]
    o_ref[...] = x_ref[...] + (1 + gate) * _dot(a_ref[...], w_ref[...])


def _oproj(a, w, lead, x, mod, gate_row, seq):
    M, K = a.shape
    D = w.shape[-1]
    tm = _pick(seq, 1024, 8)
    tn = _pick(D, 1024)
    per_b = seq // tm
    return pl.pallas_call(
        functools.partial(_oproj_kernel, gate_row=gate_row),
        grid=(M // tm, D // tn),
        in_specs=[
            pl.BlockSpec((tm, K), lambda i, j: (i, 0)),
            _wspec(lead, (K, tn), lambda i, j: (0, j)),
            pl.BlockSpec((tm, tn), lambda i, j: (i, j)),
            pl.BlockSpec((1, 9, tn), lambda i, j: (i // per_b, 0, j)),
        ],
        out_specs=pl.BlockSpec((tm, tn), lambda i, j: (i, j)),
        out_shape=jax.ShapeDtypeStruct((M, D), F32),
        compiler_params=_params(("parallel", "arbitrary")),
        name="oproj",
    )(a, w, x, mod)


def _attn_kernel(*refs, seq):
    n_g = len(ATTN_GROUPS)
    qkv = refs[:3 * n_g]
    out_ref = refs[3 * n_g]
    s_scr, p_scr, o_scr, m_scr, l_scr = refs[3 * n_g + 1:]
    T = ATTN_BLOCK
    scale = HEAD_DIM ** -0.5
    row = lax.broadcasted_iota(jnp.int32, (T, T), 0)
    col = lax.broadcasted_iota(jnp.int32, (T, T), 1)
    mask_c = col <= row
    mask_p = col >= row

    for gi, (_, dil) in enumerate(ATTN_GROUPS):
        q_ref, k_ref, v_ref = qkv[3 * gi:3 * gi + 3]
        blocks = [(r, n) for r in range(dil) for n in range(seq // dil // T)]

        def keys(n):
            return slice(max(n - 1, 0) * T, (n + 1) * T)

        def tokens(r, n):
            return pl.ds(n * T * dil + r, T, stride=dil) if dil > 1 else pl.ds(n * T, T)

        for bi, (r, n) in enumerate(blocks):
            s = _dot_nt(q_ref[0, 0, r, n * T:(n + 1) * T, :], k_ref[0, 0, r, keys(n), :]) * scale
            s_c = jnp.where(mask_c, s[:, -T:], MASK_VALUE)
            m = jnp.max(s_c, axis=-1, keepdims=True)
            s_scr[bi, :, T:] = s_c
            if n > 0:
                s_p = jnp.where(mask_p, s[:, :T], MASK_VALUE)
                m = jnp.maximum(m, jnp.max(s_p, axis=-1, keepdims=True))
                s_scr[bi, :, :T] = s_p
            m_scr[gi, tokens(r, n), :] = jnp.broadcast_to(m, (T, HEAD_DIM))
        for bi, (r, n) in enumerate(blocks):
            m = m_scr[gi, tokens(r, n), :]
            p_c = jnp.where(mask_c, jnp.exp(s_scr[bi, :, T:] - m), 0.0)
            l = jnp.sum(p_c, axis=-1, keepdims=True)
            p_scr[bi, :, T:] = p_c.astype(BF16)
            if n > 0:
                p_p = jnp.where(mask_p, jnp.exp(s_scr[bi, :, :T] - m), 0.0)
                l = l + jnp.sum(p_p, axis=-1, keepdims=True)
                p_scr[bi, :, :T] = p_p.astype(BF16)
            l_scr[gi, tokens(r, n), :] = jnp.broadcast_to(l, (T, HEAD_DIM))
        for bi, (r, n) in enumerate(blocks):
            p = p_scr[bi] if n > 0 else p_scr[bi, :, T:]
            o_scr[gi, tokens(r, n), :] = _dot(p, v_ref[0, 0, r, keys(n), :])

    for t in range(seq // T):
        rows = pl.ds(t * T, T)
        ms = [m_scr[gi, rows, :] for gi in range(n_g)]
        mx = functools.reduce(jnp.maximum, ms)
        ws = [jnp.exp(m - mx) for m in ms]
        num = sum(w * o_scr[gi, rows, :] for gi, w in enumerate(ws))
        den = sum(w * l_scr[gi, rows, :] for gi, w in enumerate(ws))
        out_ref[rows, :] = (num / den).astype(BF16)


def _attention(qs, ks, vs, batch, seq, n_heads):
    n_g = len(ATTN_GROUPS)
    in_specs, args = [], []
    for gi, (_, dil) in enumerate(ATTN_GROUPS):
        spec = pl.BlockSpec((1, 1, dil, seq // dil, HEAD_DIM), lambda b, h: (b, h, 0, 0, 0))
        in_specs += [spec, spec, spec]
        args += [qs[gi], ks[gi], vs[gi]]
    return pl.pallas_call(
        functools.partial(_attn_kernel, seq=seq),
        grid=(batch, n_heads),
        in_specs=in_specs,
        out_specs=pl.BlockSpec((seq, HEAD_DIM), lambda b, h: (b, h)),
        out_shape=jax.ShapeDtypeStruct((batch * seq, n_heads * HEAD_DIM), BF16),
        scratch_shapes=[pltpu.VMEM((seq // ATTN_BLOCK, ATTN_BLOCK, 2 * ATTN_BLOCK), F32),
                        pltpu.VMEM((seq // ATTN_BLOCK, ATTN_BLOCK, 2 * ATTN_BLOCK), BF16)]
        + [pltpu.VMEM((n_g, seq, HEAD_DIM), F32)] * 3,
        compiler_params=_params(("parallel", "parallel")),
        name="attn",
    )(*args)


def kernel(x, c, norm_g, w_ada, b_ada, w_ffn_in, w_ffn_out, hgrn_w_in, hgrn_w_out, hgrn_lb_logits, hgrn_out_gain, kv_norm_g, kv_w_ada, kv_b_ada, w_kv, k_gain, attn_w_q, attn_q_gain, attn_w_o):
    B, S, D = x.shape
    depth = w_ada.shape[0]
    n_a = hgrn_w_in.shape[0]
    n_groups = k_gain.shape[0]
    width = attn_w_o.shape[1]
    n_heads = width // HEAD_DIM
    GW = n_groups * width
    M = B * S
    assert n_groups == len(ATTN_GROUPS)
    assert all(win // dil == ATTN_BLOCK and S % (dil * ATTN_BLOCK) == 0 for win, dil in ATTN_GROUPS)

    mods = _ada(c, w_ada, b_ada).reshape(depth, B, 9, D)
    kv_mod = _ada(c, kv_w_ada[None], kv_b_ada[None]).reshape(B, 2, D)

    p = jax.nn.softmax(hgrn_lb_logits.astype(F32), axis=0)
    lb = jnp.cumsum(p, axis=0) - p[0]
    lb_floor = jnp.maximum(lb, LB_FLOOR)
    one_m_lb = 1 - lb
    w_ffn_in_b, w_ffn_out_b = w_ffn_in.astype(BF16), w_ffn_out.astype(BF16)
    hgrn_w_in_b, hgrn_w_out_b = hgrn_w_in.astype(BF16), hgrn_w_out.astype(BF16)
    attn_w_q_b, attn_w_o_b = attn_w_q.astype(BF16), attn_w_o.astype(BF16)
    w_kv_b = w_kv.astype(BF16)

    gain_spec = pl.BlockSpec((1, 1, HEAD_DIM), lambda i, j: (j, 0, 0))
    tile = lambda tn: pl.BlockSpec((_pick(S, 512, 8), tn), lambda i, j: (i, j))

    xf = x.reshape(M, D)
    ks = vs = None
    for l in range(depth):
        mod = mods[l]
        xf = _ffn(xf, norm_g[l, 0][None], mod, w_ffn_in_b, w_ffn_out_b, (l, 0), 0, S)
        if l < n_a:
            tn = _pick(D, 512)
            vec = pl.BlockSpec((1, tn), lambda i, j: (0, j))
            q, lf, k, v, sg = _proj(
                xf, norm_g[l, 1][None], mod, hgrn_w_in_b, (l,), [0, D, 2 * D, 3 * D], D, tn,
                _pick(S, 512, 8), S, 3, 4,
                [lb_floor[l][None], one_m_lb[l][None]], [vec, vec],
                [tile(tn)] * 5,
                [jax.ShapeDtypeStruct((M, D), dt) for dt in (F32, F32, F32, BF16, BF16)],
                _hgrn_epilogue, "hgrn_proj")
            a = _hgrn(q, lf, k, v, sg, hgrn_out_gain[l][None], S)
            xf = _oproj(a, hgrn_w_out_b, (l,), xf, mod, 5, S)
        else:
            jl = l - n_a
            tm = _pick(S, 512, 8)
            specs, shapes = _dilated_out(B, S, tm, n_heads)
            qs = _proj(
                xf, norm_g[l, 1][None], mod, attn_w_q_b, (jl,), [0], GW, width, tm, S, 3, 4,
                [attn_q_gain[jl][:, None, :]], [gain_spec], specs, shapes, _q_epilogue,
                "q_proj", scratch=[pltpu.VMEM((n_heads, tm, HEAD_DIM), F32)])
            a = _attention(qs, ks, vs, B, S, n_heads)
            xf = _oproj(a, attn_w_o_b, (jl,), xf, mod, 5, S)
        xf = _ffn(xf, norm_g[l, 2][None], mod, w_ffn_in_b, w_ffn_out_b, (l, 1), 2, S)
        if l == n_a - 1:
            tm = _pick(S, 512, 8)
            specs, shapes = _dilated_out(B, S, tm, n_heads)
            kv = _proj(
                xf, kv_norm_g[None], kv_mod, w_kv_b, (), [0, GW], GW, width, tm, S, 0, 1,
                [k_gain[:, None, :]], [gain_spec], specs * 2, shapes * 2, _kv_epilogue,
                "kv_proj", scratch=[pltpu.VMEM((n_heads, tm, HEAD_DIM), F32)])
            ks, vs = kv[:n_groups], kv[n_groups:]
    return xf.reshape(B, S, D)
```

```python
import functools

import numpy as np
import jax
import jax.numpy as jnp
from jax import lax
from jax.experimental import pallas as pl
from jax.experimental.pallas import tpu as pltpu

F32 = jnp.float32
BF16 = jnp.bfloat16

EPS = 1e-6
MASK_VALUE = -1e30
LB_FLOOR = 1e-30
LOG2_E = 1.4426950408889634
HEAD_DIM = 128
ATTN_GROUPS = ((128, 1), (512, 4), (2048, 16))
ATTN_BLOCK = 128
HGRN_CHUNK = 128
HGRN_DIAG = 2
HGRN_HEADS_PER_STEP = 4

VMEM_LIMIT_BYTES = 56 * 1024 * 1024


def _pick(n, target, mult=128):
    if n <= target:
        return n
    d = (target // mult) * mult
    while d >= mult:
        if n % d == 0:
            return d
        d -= mult
    raise ValueError(f"no tile for {n}")


def _params(sem):
    return pltpu.CompilerParams(dimension_semantics=sem, vmem_limit_bytes=VMEM_LIMIT_BYTES)


def _silu(x):
    return x * jax.nn.sigmoid(x)


def _dot(a, b):
    return jnp.dot(a, b, preferred_element_type=F32)


def _dot_nt(a, b):
    return lax.dot_general(a, b, (((1,), (1,)), ((), ())), preferred_element_type=F32)


NORM_ROWS = 32


def _norm_mod_into(hn_ref, x_ref, gain, shift, scale):
    gs = gain * (1 + scale)
    for r in range(0, x_ref.shape[0], NORM_ROWS):
        x = x_ref[r:r + NORM_ROWS, :]
        ms = jnp.mean(x * x, axis=-1, keepdims=True)
        hn_ref[r:r + NORM_ROWS, :] = (x * lax.rsqrt(ms + EPS) * gs + shift).astype(BF16)


def _head_norm(a, gain):
    outs = []
    for h in range(a.shape[1] // HEAD_DIM):
        ah = a[:, h * HEAD_DIM:(h + 1) * HEAD_DIM]
        ms = jnp.mean(ah * ah, axis=-1, keepdims=True)
        outs.append(ah * lax.rsqrt(ms + EPS) * gain)
    return outs[0] if len(outs) == 1 else jnp.concatenate(outs, axis=1)


def _ada_kernel(c_ref, w_ref, b_ref, o_ref):
    sc = _silu(c_ref[...]).astype(BF16)
    o_ref[0] = _dot(sc, w_ref[0].astype(BF16)) + b_ref[0]


def _ada(c, w, b):
    L, D, N = w.shape
    B = c.shape[0]
    tn = _pick(N, 1024)
    return pl.pallas_call(
        _ada_kernel,
        grid=(L, N // tn),
        in_specs=[
            pl.BlockSpec((B, D), lambda l, j: (0, 0)),
            pl.BlockSpec((1, D, tn), lambda l, j: (l, 0, j)),
            pl.BlockSpec((1, 1, tn), lambda l, j: (l, 0, j)),
        ],
        out_specs=pl.BlockSpec((1, B, tn), lambda l, j: (l, 0, j)),
        out_shape=jax.ShapeDtypeStruct((L, B, N), F32),
        compiler_params=_params(("parallel", "parallel")),
        name="ada",
    )(c, w, b.reshape(L, 1, N))


def _ffn_kernel(x_ref, g_ref, mod_ref, w_in_ref, w_out_ref, o_ref,
                hn_ref, wg_buf, wu_buf, wo_buf, sems, *, lead, sub, nj, tf, n_tiles):
    i = pl.program_id(0)
    F = nj * tf

    def copies(j, slot):
        cols = pl.ds(j * tf, tf)
        return (
            pltpu.make_async_copy(w_in_ref.at[lead + (slice(None), cols)], wg_buf.at[slot],
                                  sems.at[0, slot]),
            pltpu.make_async_copy(w_in_ref.at[lead + (slice(None), pl.ds(F + j * tf, tf))],
                                  wu_buf.at[slot], sems.at[1, slot]),
            pltpu.make_async_copy(w_out_ref.at[lead + (cols, slice(None))], wo_buf.at[slot],
                                  sems.at[2, slot]),
        )

    def start(j, slot):
        for c in copies(j, slot):
            c.start()

    @pl.when(i == 0)
    def _():
        start(0, 0)

    _norm_mod_into(hn_ref, x_ref, g_ref[...], mod_ref[0, 3 * sub:3 * sub + 1, :],
                   mod_ref[0, 3 * sub + 1:3 * sub + 2, :])
    hn = hn_ref[...]
    first = (i * nj) % 2 if nj % 2 else 0
    for j in range(nj):
        slot = (first + j) % 2
        if j + 1 < nj:
            start(j + 1, 1 - slot)
        else:
            @pl.when(i + 1 < n_tiles)
            def _():
                start(0, 1 - slot)
        for c in copies(j, slot):
            c.wait()
        hg = _dot(hn, wg_buf[slot])
        hu = _dot(hn, wu_buf[slot])
        act = (_silu(hg) * hu).astype(BF16)
        y = _dot(act, wo_buf[slot])
        if j == 0:
            o_ref[...] = y
        else:
            o_ref[...] += y
    gate = mod_ref[0, 3 * sub + 2:3 * sub + 3, :]
    o_ref[...] = x_ref[...] + (0.5 * (1 + gate)) * o_ref[...]


def _wspec(lead, block, index_map):
    return pl.BlockSpec((None,) * len(lead) + block, lambda i, j: lead + index_map(i, j))


def _ffn(x, gain, mod, w_in, w_out, lead, sub, seq):
    M, D = x.shape
    F = w_out.shape[-2]
    tm = _pick(seq, 512, 8)
    tf = _pick(F, 512)
    nj = F // tf
    per_b = seq // tm
    return pl.pallas_call(
        functools.partial(_ffn_kernel, lead=lead, sub=sub, nj=nj, tf=tf, n_tiles=M // tm),
        grid=(M // tm,),
        in_specs=[
            pl.BlockSpec((tm, D), lambda i: (i, 0)),
            pl.BlockSpec((1, D), lambda i: (0, 0)),
            pl.BlockSpec((1, 9, D), lambda i: (i // per_b, 0, 0)),
            pl.BlockSpec(memory_space=pl.ANY),
            pl.BlockSpec(memory_space=pl.ANY),
        ],
        out_specs=pl.BlockSpec((tm, D), lambda i: (i, 0)),
        out_shape=jax.ShapeDtypeStruct((M, D), F32),
        scratch_shapes=[pltpu.VMEM((tm, D), BF16),
                        pltpu.VMEM((2, D, tf), BF16), pltpu.VMEM((2, D, tf), BF16),
                        pltpu.VMEM((2, tf, D), BF16), pltpu.SemaphoreType.DMA((3, 2))],
        compiler_params=_params(("arbitrary",)),
        name="ffn",
    )(x, gain, mod, w_in, w_out)


def _proj_kernel(*refs, n_w, n_extra, n_out, shift_row, scale_row, epilogue):
    x_ref, g_ref, mod_ref = refs[:3]
    w_refs = refs[3:3 + n_w]
    extra_refs = refs[3 + n_w:3 + n_w + n_extra]
    out_refs = refs[3 + n_w + n_extra:3 + n_w + n_extra + n_out]
    hn_ref = refs[3 + n_w + n_extra + n_out]
    scratch_refs = refs[4 + n_w + n_extra + n_out:]

    @pl.when(pl.program_id(1) == 0)
    def _():
        _norm_mod_into(hn_ref, x_ref, g_ref[...], mod_ref[0, shift_row:shift_row + 1, :],
                       mod_ref[0, scale_row:scale_row + 1, :])

    hn = hn_ref[...]
    accs = [_dot(hn, w[...]) for w in w_refs]
    epilogue(accs, extra_refs, out_refs, scratch_refs)


def _proj(x, gain, mod, w, lead, col_offsets, n_cols, tn, tm, seq, shift_row, scale_row,
          extras, extra_specs, out_specs, out_shapes, epilogue, name, scratch=()):
    M, D = x.shape
    per_b = seq // tm
    n_mod = mod.shape[1]
    w_specs = [
        _wspec(lead, (D, tn), functools.partial(lambda i, j, o: (0, o + j), o=off // tn))
        for off in col_offsets
    ]
    return pl.pallas_call(
        functools.partial(_proj_kernel, n_w=len(col_offsets), n_extra=len(extras),
                          n_out=len(out_specs), shift_row=shift_row, scale_row=scale_row,
                          epilogue=epilogue),
        grid=(M // tm, n_cols // tn),
        in_specs=[
            pl.BlockSpec((tm, D), lambda i, j: (i, 0)),
            pl.BlockSpec((1, D), lambda i, j: (0, 0)),
            pl.BlockSpec((1, n_mod, D), lambda i, j: (i // per_b, 0, 0)),
        ] + w_specs + extra_specs,
        out_specs=out_specs,
        out_shape=out_shapes,
        scratch_shapes=[pltpu.VMEM((tm, D), BF16), *scratch],
        compiler_params=_params(("parallel", "arbitrary")),
        name=name,
    )(x, gain, mod, *([w] * len(col_offsets)), *extras)


def _hgrn_epilogue(accs, extra_refs, out_refs, scratch_refs):
    q, ff, v, g = accs
    lb_floor, one_m_lb = (r[...] for r in extra_refs)
    q_ref, lf_ref, k_ref, v_ref, sg_ref = out_refs
    q_ref[...] = _silu(q)
    e = jnp.exp(-jnp.abs(ff))
    r = 1.0 / (1.0 + e)
    pos = ff >= 0
    lf_ref[...] = jnp.log(lb_floor + one_m_lb * jnp.where(pos, r, e * r)) * LOG2_E
    k_ref[...] = one_m_lb * jnp.where(pos, e * r, r)
    v_ref[...] = v.astype(BF16)
    sg_ref[...] = _silu(g).astype(BF16)


def _store_dilated(a, scr_ref, out_ref, dil):
    rows, width = a.shape
    for h in range(width // HEAD_DIM):
        ah = a[:, h * HEAD_DIM:(h + 1) * HEAD_DIM]
        if dil == 1:
            out_ref[0, h, 0] = ah.astype(BF16)
        else:
            scr_ref[h] = ah
            for r in range(dil):
                out_ref[0, h, r] = scr_ref[h, pl.ds(r, rows // dil, stride=dil), :].astype(BF16)


def _q_epilogue(accs, extra_refs, out_refs, scratch_refs):
    for gi, (_, dil) in enumerate(ATTN_GROUPS):
        @pl.when(pl.program_id(1) == gi)
        def _(gi=gi, dil=dil):
            qn = _head_norm(accs[0], extra_refs[0][0])
            _store_dilated(qn, scratch_refs[0], out_refs[gi], dil)


def _kv_epilogue(accs, extra_refs, out_refs, scratch_refs):
    n = len(ATTN_GROUPS)
    for gi, (_, dil) in enumerate(ATTN_GROUPS):
        @pl.when(pl.program_id(1) == gi)
        def _(gi=gi, dil=dil):
            kn = _head_norm(accs[0], extra_refs[0][0])
            _store_dilated(kn, scratch_refs[0], out_refs[gi], dil)
            _store_dilated(accs[1], scratch_refs[0], out_refs[n + gi], dil)


def _dilated_out(batch, seq, tm, n_heads):
    per_b = seq // tm
    specs, shapes = [], []
    for _, dil in ATTN_GROUPS:
        specs.append(pl.BlockSpec((1, n_heads, dil, tm // dil, HEAD_DIM),
                                  lambda i, j: (i // per_b, 0, 0, i % per_b, 0)))
        shapes.append(jax.ShapeDtypeStruct((batch, n_heads, dil, seq // dil, HEAD_DIM), BF16))
    return specs, shapes


def _hgrn_levels():
    P, out = 2 * HGRN_DIAG, []
    while P <= HGRN_CHUNK:
        out.append(P)
        P *= 2
    return out


def _hgrn_masks():
    t = np.arange(HGRN_CHUNK)
    rows = []
    for P in _hgrn_levels():
        same = (t[:, None] // P) == (t[None, :] // P)
        rows.append(same & ((t[:, None] % P) >= P // 2) & ((t[None, :] % P) < P // 2))
    return jnp.asarray(np.stack(rows), F32)


def _dot_tn(a, b):
    return lax.dot_general(a, b, (((0,), (0,)), ((), ())), preferred_element_type=F32)


def _hgrn_chunk(qs, lf, k, v, ST, tri, ones, delta, same_diag_block, tok, mask_ref):
    C, c = HGRN_CHUNK, HGRN_DIAG
    lf_hi = lf.astype(BF16)
    lf_lo = (lf - lf_hi.astype(F32)).astype(BF16)
    G = _dot(tri, lf_hi) + _dot(tri, lf_lo)

    W = [(qs * k).astype(BF16)]
    for d in range(1, c):
        ks = pltpu.roll(k, d, axis=0)
        Gs = pltpu.roll(G, d, axis=0)
        W.append((qs * ks * jnp.exp2(G - Gs)).astype(BF16))
    R = _dot(jnp.concatenate(W, axis=0), ones)
    A = jnp.zeros((C, C), F32)
    for d in range(c):
        A = jnp.where(delta == d, R[d * C:(d + 1) * C], A)
    A = jnp.where(same_diag_block, A, 0.0)

    for li, P in enumerate(_hgrn_levels()):
        half = P // 2
        if half % 8 == 0:
            src, expo = [], []
            for b in range(C // P):
                lo, mid, hi = b * P, b * P + half, (b + 1) * P
                g_mid = G[mid - 1:mid, :]
                src += [k[lo:mid], qs[mid:hi]]
                expo += [g_mid - G[lo:mid], G[mid:hi] - g_mid]
            Z = jnp.concatenate(src, axis=0) * jnp.exp2(jnp.concatenate(expo, axis=0))
        else:
            Ge = jnp.concatenate(
                [jnp.broadcast_to(G[b * P + half - 1:b * P + half, :], (P, HEAD_DIM))
                 for b in range(C // P)], axis=0)
            second = (tok % P) >= half
            Z = jnp.where(second, qs, k) * jnp.exp2(jnp.where(second, 1.0, -1.0) * (G - Ge))
        Z = Z.astype(BF16)
        A = A + mask_ref[li] * _dot_nt(Z, Z)

    o = _dot(A.astype(BF16), v) + _dot_nt((qs * jnp.exp2(G)).astype(BF16), ST.astype(BF16))
    g_last = G[C - 1:C, :]
    k_dec = (k * jnp.exp2(g_last - G)).astype(BF16)
    ST_new = ST * jnp.exp2(g_last) + _dot_tn(v, k_dec)
    return o, ST_new


def _hgrn_kernel(q_ref, lf_ref, k_ref, v_ref, sg_ref, gain_ref, mask_ref, o_ref, st_ref,
                 *, n_chunks, n_heads):
    C, c = HGRN_CHUNK, HGRN_DIAG
    row = lax.broadcasted_iota(jnp.int32, (C, C), 0)
    col = lax.broadcasted_iota(jnp.int32, (C, C), 1)
    tri = (col <= row).astype(BF16)
    ones = jnp.ones((HEAD_DIM, C), BF16)
    delta = row - col
    same_diag_block = (row // c) == (col // c)
    tok = lax.broadcasted_iota(jnp.int32, (C, 1), 0)
    gain = gain_ref[...]
    st_ref[...] = jnp.zeros_like(st_ref)

    def chunk(ci, carry):
        rows = pl.ds(pl.multiple_of(ci * C, C), C)
        for h in range(n_heads):
            sl = slice(h * HEAD_DIM, (h + 1) * HEAD_DIM)
            o, S_new = _hgrn_chunk(q_ref[rows, sl], lf_ref[rows, sl], k_ref[rows, sl],
                                   v_ref[rows, sl], st_ref[h], tri, ones, delta,
                                   same_diag_block, tok, mask_ref)
            st_ref[h] = S_new
            ms = jnp.mean(o * o, axis=-1, keepdims=True)
            on = o * lax.rsqrt(ms + EPS) * gain
            o_ref[rows, sl] = (on * sg_ref[rows, sl].astype(F32)).astype(BF16)
        return carry

    lax.fori_loop(0, n_chunks, chunk, 0, unroll=4)


def _hgrn(q, lf, k, v, sg, out_gain, seq):
    M, D = q.shape
    hb = min(HGRN_HEADS_PER_STEP, D // HEAD_DIM)
    masks = _hgrn_masks()
    blk = lambda: pl.BlockSpec((seq, hb * HEAD_DIM), lambda b, h: (b, h))
    return pl.pallas_call(
        functools.partial(_hgrn_kernel, n_chunks=seq // HGRN_CHUNK, n_heads=hb),
        grid=(M // seq, D // (hb * HEAD_DIM)),
        in_specs=[blk(), blk(), blk(), blk(), blk(),
                  pl.BlockSpec((1, HEAD_DIM), lambda b, h: (0, 0)),
                  pl.BlockSpec(masks.shape, lambda b, h: (0, 0, 0))],
        out_specs=blk(),
        out_shape=jax.ShapeDtypeStruct((M, D), BF16),
        scratch_shapes=[pltpu.VMEM((hb, HEAD_DIM, HEAD_DIM), F32)],
        compiler_params=_params(("parallel", "parallel")),
        name="hgrn",
    )(q, lf, k, v, sg, out_gain, masks)


def _oproj_kernel(a_ref, w_ref, x_ref, mod_ref, o_ref, *, gate_row):
    gate = mod_ref[0, gate_row:gate_row + 1, :]
    o_ref[...] = x_ref[...] + (1 + gate) * _dot(a_ref[...], w_ref[...])


def _oproj(a, w, lead, x, mod, gate_row, seq):
    M, K = a.shape
    D = w.shape[-1]
    tm = _pick(seq, 1024, 8)
    tn = _pick(D, 1024)
    per_b = seq // tm
    return pl.pallas_call(
        functools.partial(_oproj_kernel, gate_row=gate_row),
        grid=(M // tm, D // tn),
        in_specs=[
            pl.BlockSpec((tm, K), lambda i, j: (i, 0)),
            _wspec(lead, (K, tn), lambda i, j: (0, j)),
            pl.BlockSpec((tm, tn), lambda i, j: (i, j)),
            pl.BlockSpec((1, 9, tn), lambda i, j: (i // per_b, 0, j)),
        ],
        out_specs=pl.BlockSpec((tm, tn), lambda i, j: (i, j)),
        out_shape=jax.ShapeDtypeStruct((M, D), F32),
        compiler_params=_params(("parallel", "arbitrary")),
        name="oproj",
    )(a, w, x, mod)


def _attn_kernel(*refs, seq):
    n_g = len(ATTN_GROUPS)
    qkv = refs[:3 * n_g]
    out_ref = refs[3 * n_g]
    s_scr, p_scr, o_scr, m_scr, l_scr = refs[3 * n_g + 1:]
    T = ATTN_BLOCK
    scale = HEAD_DIM ** -0.5
    row = lax.broadcasted_iota(jnp.int32, (T, T), 0)
    col = lax.broadcasted_iota(jnp.int32, (T, T), 1)
    mask_c = col <= row
    mask_p = col >= row

    for gi, (_, dil) in enumerate(ATTN_GROUPS):
        q_ref, k_ref, v_ref = qkv[3 * gi:3 * gi + 3]
        blocks = [(r, n) for r in range(dil) for n in range(seq // dil // T)]

        def keys(n):
            return slice(max(n - 1, 0) * T, (n + 1) * T)

        def tokens(r, n):
            return pl.ds(n * T * dil + r, T, stride=dil) if dil > 1 else pl.ds(n * T, T)

        for bi, (r, n) in enumerate(blocks):
            s = _dot_nt(q_ref[0, 0, r, n * T:(n + 1) * T, :], k_ref[0, 0, r, keys(n), :]) * scale
            s_c = jnp.where(mask_c, s[:, -T:], MASK_VALUE)
            m = jnp.max(s_c, axis=-1, keepdims=True)
            s_scr[bi, :, T:] = s_c
            if n > 0:
                s_p = jnp.where(mask_p, s[:, :T], MASK_VALUE)
                m = jnp.maximum(m, jnp.max(s_p, axis=-1, keepdims=True))
                s_scr[bi, :, :T] = s_p
            m_scr[gi, tokens(r, n), :] = jnp.broadcast_to(m, (T, HEAD_DIM))
        for bi, (r, n) in enumerate(blocks):
            m = m_scr[gi, tokens(r, n), :]
            p_c = jnp.where(mask_c, jnp.exp(s_scr[bi, :, T:] - m), 0.0)
            l = jnp.sum(p_c, axis=-1, keepdims=True)
            p_scr[bi, :, T:] = p_c.astype(BF16)
            if n > 0:
                p_p = jnp.where(mask_p, jnp.exp(s_scr[bi, :, :T] - m), 0.0)
                l = l + jnp.sum(p_p, axis=-1, keepdims=True)
                p_scr[bi, :, :T] = p_p.astype(BF16)
            l_scr[gi, tokens(r, n), :] = jnp.broadcast_to(l, (T, HEAD_DIM))
        for bi, (r, n) in enumerate(blocks):
            p = p_scr[bi] if n > 0 else p_scr[bi, :, T:]
            o_scr[gi, tokens(r, n), :] = _dot(p, v_ref[0, 0, r, keys(n), :])

    for t in range(seq // T):
        rows = pl.ds(t * T, T)
        ms = [m_scr[gi, rows, :] for gi in range(n_g)]
        mx = functools.reduce(jnp.maximum, ms)
        ws = [jnp.exp(m - mx) for m in ms]
        num = sum(w * o_scr[gi, rows, :] for gi, w in enumerate(ws))
        den = sum(w * l_scr[gi, rows, :] for gi, w in enumerate(ws))
        out_ref[rows, :] = (num / den).astype(BF16)


def _attention(qs, ks, vs, batch, seq, n_heads):
    n_g = len(ATTN_GROUPS)
    in_specs, args = [], []
    for gi, (_, dil) in enumerate(ATTN_GROUPS):
        spec = pl.BlockSpec((1, 1, dil, seq // dil, HEAD_DIM), lambda b, h: (b, h, 0, 0, 0))
        in_specs += [spec, spec, spec]
        args += [qs[gi], ks[gi], vs[gi]]
    return pl.pallas_call(
        functools.partial(_attn_kernel, seq=seq),
        grid=(batch, n_heads),
        in_specs=in_specs,
        out_specs=pl.BlockSpec((seq, HEAD_DIM), lambda b, h: (b, h)),
        out_shape=jax.ShapeDtypeStruct((batch * seq, n_heads * HEAD_DIM), BF16),
        scratch_shapes=[pltpu.VMEM((seq // ATTN_BLOCK, ATTN_BLOCK, 2 * ATTN_BLOCK), F32),
                        pltpu.VMEM((seq // ATTN_BLOCK, ATTN_BLOCK, 2 * ATTN_BLOCK), BF16)]
        + [pltpu.VMEM((n_g, seq, HEAD_DIM), F32)] * 3,
        compiler_params=_params(("parallel", "parallel")),
        name="attn",
    )(*args)


def kernel(x, c, norm_g, w_ada, b_ada, w_ffn_in, w_ffn_out, hgrn_w_in, hgrn_w_out, hgrn_lb_logits, hgrn_out_gain, kv_norm_g, kv_w_ada, kv_b_ada, w_kv, k_gain, attn_w_q, attn_q_gain, attn_w_o):
    B, S, D = x.shape
    depth = w_ada.shape[0]
    n_a = hgrn_w_in.shape[0]
    n_groups = k_gain.shape[0]
    width = attn_w_o.shape[1]
    n_heads = width // HEAD_DIM
    GW = n_groups * width
    M = B * S
    assert n_groups == len(ATTN_GROUPS)
    assert all(win // dil == ATTN_BLOCK and S % (dil * ATTN_BLOCK) == 0 for win, dil in ATTN_GROUPS)

    mods = _ada(c, w_ada, b_ada).reshape(depth, B, 9, D)
    kv_mod = _ada(c, kv_w_ada[None], kv_b_ada[None]).reshape(B, 2, D)

    p = jax.nn.softmax(hgrn_lb_logits.astype(F32), axis=0)
    lb = jnp.cumsum(p, axis=0) - p[0]
    lb_floor = jnp.maximum(lb, LB_FLOOR)
    one_m_lb = 1 - lb
    w_ffn_in_b, w_ffn_out_b = w_ffn_in.astype(BF16), w_ffn_out.astype(BF16)
    hgrn_w_in_b, hgrn_w_out_b = hgrn_w_in.astype(BF16), hgrn_w_out.astype(BF16)
    attn_w_q_b, attn_w_o_b = attn_w_q.astype(BF16), attn_w_o.astype(BF16)
    w_kv_b = w_kv.astype(BF16)

    gain_spec = pl.BlockSpec((1, 1, HEAD_DIM), lambda i, j: (j, 0, 0))
    tile = lambda tn: pl.BlockSpec((_pick(S, 512, 8), tn), lambda i, j: (i, j))

    xf = x.reshape(M, D)
    ks = vs = None
    for l in range(depth):
        mod = mods[l]
        xf = _ffn(xf, norm_g[l, 0][None], mod, w_ffn_in_b, w_ffn_out_b, (l, 0), 0, S)
        if l < n_a:
            tn = _pick(D, 512)
            vec = pl.BlockSpec((1, tn), lambda i, j: (0, j))
            q, lf, k, v, sg = _proj(
                xf, norm_g[l, 1][None], mod, hgrn_w_in_b, (l,), [0, D, 2 * D, 3 * D], D, tn,
                _pick(S, 512, 8), S, 3, 4,
                [lb_floor[l][None], one_m_lb[l][None]], [vec, vec],
                [tile(tn)] * 5,
                [jax.ShapeDtypeStruct((M, D), dt) for dt in (F32, F32, F32, BF16, BF16)],
                _hgrn_epilogue, "hgrn_proj")
            a = _hgrn(q, lf, k, v, sg, hgrn_out_gain[l][None], S)
            xf = _oproj(a, hgrn_w_out_b, (l,), xf, mod, 5, S)
        else:
            jl = l - n_a
            tm = _pick(S, 512, 8)
            specs, shapes = _dilated_out(B, S, tm, n_heads)
            qs = _proj(
                xf, norm_g[l, 1][None], mod, attn_w_q_b, (jl,), [0], GW, width, tm, S, 3, 4,
                [attn_q_gain[jl][:, None, :]], [gain_spec], specs, shapes, _q_epilogue,
                "q_proj", scratch=[pltpu.VMEM((n_heads, tm, HEAD_DIM), F32)])
            a = _attention(qs, ks, vs, B, S, n_heads)
            xf = _oproj(a, attn_w_o_b, (jl,), xf, mod, 5, S)
        xf = _ffn(xf, norm_g[l, 2][None], mod, w_ffn_in_b, w_ffn_out_b, (l, 1), 2, S)
        if l == n_a - 1:
            tm = _pick(S, 512, 8)
            specs, shapes = _dilated_out(B, S, tm, n_heads)
            kv = _proj(
                xf, kv_norm_g[None], kv_mod, w_kv_b, (), [0, GW], GW, width, tm, S, 0, 1,
                [k_gain[:, None, :]], [gain_spec], specs * 2, shapes * 2, _kv_epilogue,
                "kv_proj", scratch=[pltpu.VMEM((n_heads, tm, HEAD_DIM), F32)])
            ks, vs = kv[:n_groups], kv[n_groups:]
    return xf.reshape(B, S, D)
```

```python
import functools

import numpy as np
import jax
import jax.numpy as jnp
from jax import lax
from jax.experimental import pallas as pl
from jax.experimental.pallas import tpu as pltpu

F32 = jnp.float32
BF16 = jnp.bfloat16

EPS = 1e-6
MASK_VALUE = -1e30
LB_FLOOR = 1e-30
LOG2_E = 1.4426950408889634
HEAD_DIM = 128
ATTN_GROUPS = ((128, 1), (512, 4), (2048, 16))
ATTN_BLOCK = 128
HGRN_CHUNK = 128
HGRN_DIAG = 2
HGRN_HEADS_PER_STEP = 4

VMEM_LIMIT_BYTES = 56 * 1024 * 1024


def _pick(n, target, mult=128):
    if n <= target:
        return n
    d = (target // mult) * mult
    while d >= mult:
        if n % d == 0:
            return d
        d -= mult
    raise ValueError(f"no tile for {n}")


def _params(sem):
    return pltpu.CompilerParams(dimension_semantics=sem, vmem_limit_bytes=VMEM_LIMIT_BYTES)


def _silu(x):
    return x * (0.5 + 0.5 * jnp.tanh(0.5 * x))


def _dot(a, b):
    return jnp.dot(a, b, preferred_element_type=F32)


def _dot_nt(a, b):
    return lax.dot_general(a, b, (((1,), (1,)), ((), ())), preferred_element_type=F32)


NORM_ROWS = 32


def _norm_mod_into(hn_ref, x_ref, gain, shift, scale):
    gs = gain * (1 + scale)
    for r in range(0, x_ref.shape[0], NORM_ROWS):
        x = x_ref[r:r + NORM_ROWS, :]
        ms = jnp.mean(x * x, axis=-1, keepdims=True)
        hn_ref[r:r + NORM_ROWS, :] = (x * lax.rsqrt(ms + EPS) * gs + shift).astype(BF16)


def _head_norm(a, gain):
    outs = []
    for h in range(a.shape[1] // HEAD_DIM):
        ah = a[:, h * HEAD_DIM:(h + 1) * HEAD_DIM]
        ms = jnp.mean(ah * ah, axis=-1, keepdims=True)
        outs.append(ah * lax.rsqrt(ms + EPS) * gain)
    return outs[0] if len(outs) == 1 else jnp.concatenate(outs, axis=1)


def _ada_kernel(c_ref, w_ref, b_ref, o_ref):
    sc = _silu(c_ref[...]).astype(BF16)
    o_ref[0] = _dot(sc, w_ref[0].astype(BF16)) + b_ref[0]


def _ada(c, w, b):
    L, D, N = w.shape
    B = c.shape[0]
    tn = _pick(N, 1024)
    return pl.pallas_call(
        _ada_kernel,
        grid=(L, N // tn),
        in_specs=[
            pl.BlockSpec((B, D), lambda l, j: (0, 0)),
            pl.BlockSpec((1, D, tn), lambda l, j: (l, 0, j)),
            pl.BlockSpec((1, 1, tn), lambda l, j: (l, 0, j)),
        ],
        out_specs=pl.BlockSpec((1, B, tn), lambda l, j: (l, 0, j)),
        out_shape=jax.ShapeDtypeStruct((L, B, N), F32),
        compiler_params=_params(("parallel", "parallel")),
        name="ada",
    )(c, w, b.reshape(L, 1, N))


def _ffn_kernel(x_ref, g_ref, mod_ref, w_in_ref, w_out_ref, o_ref,
                hn_ref, wg_buf, wu_buf, wo_buf, sems, *, lead, sub, nj, tf, n_tiles):
    i = pl.program_id(0)
    F = nj * tf

    def copies(j, slot):
        cols = pl.ds(j * tf, tf)
        return (
            pltpu.make_async_copy(w_in_ref.at[lead + (slice(None), cols)], wg_buf.at[slot],
                                  sems.at[0, slot]),
            pltpu.make_async_copy(w_in_ref.at[lead + (slice(None), pl.ds(F + j * tf, tf))],
                                  wu_buf.at[slot], sems.at[1, slot]),
            pltpu.make_async_copy(w_out_ref.at[lead + (cols, slice(None))], wo_buf.at[slot],
                                  sems.at[2, slot]),
        )

    def start(j, slot):
        for c in copies(j, slot):
            c.start()

    @pl.when(i == 0)
    def _():
        start(0, 0)

    _norm_mod_into(hn_ref, x_ref, g_ref[...], mod_ref[0, 3 * sub:3 * sub + 1, :],
                   mod_ref[0, 3 * sub + 1:3 * sub + 2, :])
    hn = hn_ref[...]
    first = (i * nj) % 2 if nj % 2 else 0
    for j in range(nj):
        slot = (first + j) % 2
        if j + 1 < nj:
            start(j + 1, 1 - slot)
        else:
            @pl.when(i + 1 < n_tiles)
            def _():
                start(0, 1 - slot)
        for c in copies(j, slot):
            c.wait()
        hg = _dot(hn, wg_buf[slot])
        hu = _dot(hn, wu_buf[slot])
        act = (_silu(hg) * hu).astype(BF16)
        y = _dot(act, wo_buf[slot])
        if j == 0:
            o_ref[...] = y
        else:
            o_ref[...] += y
    gate = mod_ref[0, 3 * sub + 2:3 * sub + 3, :]
    o_ref[...] = x_ref[...] + (0.5 * (1 + gate)) * o_ref[...]


def _wspec(lead, block, index_map):
    return pl.BlockSpec((None,) * len(lead) + block, lambda i, j: lead + index_map(i, j))


def _ffn(x, gain, mod, w_in, w_out, lead, sub, seq):
    M, D = x.shape
    F = w_out.shape[-2]
    tm = _pick(seq, 512, 8)
    tf = _pick(F, 512)
    nj = F // tf
    per_b = seq // tm
    return pl.pallas_call(
        functools.partial(_ffn_kernel, lead=lead, sub=sub, nj=nj, tf=tf, n_tiles=M // tm),
        grid=(M // tm,),
        in_specs=[
            pl.BlockSpec((tm, D), lambda i: (i, 0)),
            pl.BlockSpec((1, D), lambda i: (0, 0)),
            pl.BlockSpec((1, 9, D), lambda i: (i // per_b, 0, 0)),
            pl.BlockSpec(memory_space=pl.ANY),
            pl.BlockSpec(memory_space=pl.ANY),
        ],
        out_specs=pl.BlockSpec((tm, D), lambda i: (i, 0)),
        out_shape=jax.ShapeDtypeStruct((M, D), F32),
        scratch_shapes=[pltpu.VMEM((tm, D), BF16),
                        pltpu.VMEM((2, D, tf), BF16), pltpu.VMEM((2, D, tf), BF16),
                        pltpu.VMEM((2, tf, D), BF16), pltpu.SemaphoreType.DMA((3, 2))],
        compiler_params=_params(("arbitrary",)),
        name="ffn",
    )(x, gain, mod, w_in, w_out)


def _proj_kernel(*refs, n_w, n_extra, n_out, shift_row, scale_row, epilogue):
    x_ref, g_ref, mod_ref = refs[:3]
    w_refs = refs[3:3 + n_w]
    extra_refs = refs[3 + n_w:3 + n_w + n_extra]
    out_refs = refs[3 + n_w + n_extra:3 + n_w + n_extra + n_out]
    hn_ref = refs[3 + n_w + n_extra + n_out]
    scratch_refs = refs[4 + n_w + n_extra + n_out:]

    @pl.when(pl.program_id(1) == 0)
    def _():
        _norm_mod_into(hn_ref, x_ref, g_ref[...], mod_ref[0, shift_row:shift_row + 1, :],
                       mod_ref[0, scale_row:scale_row + 1, :])

    hn = hn_ref[...]
    accs = [_dot(hn, w[...]) for w in w_refs]
    epilogue(accs, extra_refs, out_refs, scratch_refs)


def _proj(x, gain, mod, w, lead, col_offsets, n_cols, tn, tm, seq, shift_row, scale_row,
          extras, extra_specs, out_specs, out_shapes, epilogue, name, scratch=()):
    M, D = x.shape
    per_b = seq // tm
    n_mod = mod.shape[1]
    w_specs = [
        _wspec(lead, (D, tn), functools.partial(lambda i, j, o: (0, o + j), o=off // tn))
        for off in col_offsets
    ]
    return pl.pallas_call(
        functools.partial(_proj_kernel, n_w=len(col_offsets), n_extra=len(extras),
                          n_out=len(out_specs), shift_row=shift_row, scale_row=scale_row,
                          epilogue=epilogue),
        grid=(M // tm, n_cols // tn),
        in_specs=[
            pl.BlockSpec((tm, D), lambda i, j: (i, 0)),
            pl.BlockSpec((1, D), lambda i, j: (0, 0)),
            pl.BlockSpec((1, n_mod, D), lambda i, j: (i // per_b, 0, 0)),
        ] + w_specs + extra_specs,
        out_specs=out_specs,
        out_shape=out_shapes,
        scratch_shapes=[pltpu.VMEM((tm, D), BF16), *scratch],
        compiler_params=_params(("parallel", "arbitrary")),
        name=name,
    )(x, gain, mod, *([w] * len(col_offsets)), *extras)


def _hgrn_epilogue(accs, extra_refs, out_refs, scratch_refs):
    ff, q, g, v = accs
    lb_floor, one_m_lb = (r[...] for r in extra_refs)
    q_ref, lf_ref, k_ref, v_ref, sg_ref = out_refs
    e = jnp.exp(-jnp.abs(ff))
    r = 1.0 / (1.0 + e)
    pos = ff >= 0
    lf_ref[...] = jnp.log(lb_floor + one_m_lb * jnp.where(pos, r, e * r)) * LOG2_E
    k_ref[...] = one_m_lb * jnp.where(pos, e * r, r)
    q_ref[...] = _silu(q)
    sg_ref[...] = _silu(g).astype(BF16)
    v_ref[...] = v.astype(BF16)


def _store_dilated(a, scr_ref, out_ref, dil):
    rows, width = a.shape
    for h in range(width // HEAD_DIM):
        ah = a[:, h * HEAD_DIM:(h + 1) * HEAD_DIM]
        if dil == 1:
            out_ref[0, h, 0] = ah.astype(BF16)
        else:
            scr_ref[h] = ah
            for r in range(dil):
                out_ref[0, h, r] = scr_ref[h, pl.ds(r, rows // dil, stride=dil), :].astype(BF16)


def _group_proj_kernel(x_ref, g_ref, mod_ref, gain_ref, w_ref, *refs, lead, col_offsets, width,
                       shift_row, scale_row, n_tiles):
    n_g, n_w = len(ATTN_GROUPS), len(col_offsets)
    out_refs = refs[:n_w * n_g]
    hn_ref, w_buf, scr_ref, sems = refs[n_w * n_g:]
    i = pl.program_id(0)
    order = sorted(range(n_g), key=lambda gi: -ATTN_GROUPS[gi][1])

    def copies(c, slot):
        return [pltpu.make_async_copy(
            w_ref.at[lead + (slice(None), pl.ds(off + order[c] * width, width))],
            w_buf.at[slot, k], sems.at[k, slot]) for k, off in enumerate(col_offsets)]

    def start(c, slot):
        for cp in copies(c, slot):
            cp.start()

    @pl.when(i == 0)
    def _():
        start(0, 0)

    _norm_mod_into(hn_ref, x_ref, g_ref[...], mod_ref[0, shift_row:shift_row + 1, :],
                   mod_ref[0, scale_row:scale_row + 1, :])
    hn = hn_ref[...]
    first = (i * n_g) % 2 if n_g % 2 else 0
    for c, gi in enumerate(order):
        slot = (first + c) % 2
        if c + 1 < n_g:
            start(c + 1, 1 - slot)
        else:
            @pl.when(i + 1 < n_tiles)
            def _():
                start(0, 1 - slot)
        for cp in copies(c, slot):
            cp.wait()
        dil = ATTN_GROUPS[gi][1]
        for k in range(n_w):
            a = _dot(hn, w_buf[slot, k])
            if k == 0:
                a = _head_norm(a, gain_ref[gi])
            _store_dilated(a, scr_ref.at[k], out_refs[k * n_g + gi], dil)


def _group_proj(x, gain, mod, w, lead, col_offsets, head_gain, batch, seq, n_heads,
                shift_row, scale_row, name):
    M, D = x.shape
    n_g, n_w = len(ATTN_GROUPS), len(col_offsets)
    width = n_heads * HEAD_DIM
    tm = _pick(seq, 512, 8 * max(d for _, d in ATTN_GROUPS))
    per_b = seq // tm
    n_mod = mod.shape[1]
    out_specs, out_shapes = [], []
    for _ in range(n_w):
        for _, dil in ATTN_GROUPS:
            out_specs.append(pl.BlockSpec((1, n_heads, dil, tm // dil, HEAD_DIM),
                                          lambda i: (i // per_b, 0, 0, i % per_b, 0)))
            out_shapes.append(
                jax.ShapeDtypeStruct((batch, n_heads, dil, seq // dil, HEAD_DIM), BF16))
    outs = pl.pallas_call(
        functools.partial(_group_proj_kernel, lead=lead, col_offsets=tuple(col_offsets),
                          width=width, shift_row=shift_row, scale_row=scale_row,
                          n_tiles=M // tm),
        grid=(M // tm,),
        in_specs=[
            pl.BlockSpec((tm, D), lambda i: (i, 0)),
            pl.BlockSpec((1, D), lambda i: (0, 0)),
            pl.BlockSpec((1, n_mod, D), lambda i: (i // per_b, 0, 0)),
            pl.BlockSpec((n_g, 1, HEAD_DIM), lambda i: (0, 0, 0)),
            pl.BlockSpec(memory_space=pl.ANY),
        ],
        out_specs=out_specs,
        out_shape=out_shapes,
        scratch_shapes=[pltpu.VMEM((tm, D), BF16), pltpu.VMEM((2, n_w, D, width), BF16),
                        pltpu.VMEM((n_w, n_heads, tm, HEAD_DIM), F32),
                        pltpu.SemaphoreType.DMA((n_w, 2))],
        compiler_params=_params(("arbitrary",)),
        name=name,
    )(x, gain, mod, head_gain[:, None, :], w)
    return [outs[k * n_g:(k + 1) * n_g] for k in range(n_w)]


def _hgrn_levels():
    P, out = 2 * HGRN_DIAG, []
    while P <= HGRN_CHUNK:
        out.append(P)
        P *= 2
    return out


def _hgrn_masks():
    t = np.arange(HGRN_CHUNK)
    rows = []
    for P in _hgrn_levels():
        same = (t[:, None] // P) == (t[None, :] // P)
        rows.append(same & ((t[:, None] % P) >= P // 2) & ((t[None, :] % P) < P // 2))
    return jnp.asarray(np.stack(rows), F32)


def _dot_tn(a, b):
    return lax.dot_general(a, b, (((0,), (0,)), ((), ())), preferred_element_type=F32)


def _hgrn_chunk(qs, lf, k, v, ST, tri, ones, delta, same_diag_block, tok, mask_ref):
    C, c = HGRN_CHUNK, HGRN_DIAG
    lf_hi = lf.astype(BF16)
    lf_lo = (lf - lf_hi.astype(F32)).astype(BF16)
    G = _dot(tri, lf_hi) + _dot(tri, lf_lo)

    W = [(qs * k).astype(BF16)]
    for d in range(1, c):
        ks = pltpu.roll(k, d, axis=0)
        Gs = pltpu.roll(G, d, axis=0)
        W.append((qs * ks * jnp.exp2(G - Gs)).astype(BF16))
    R = _dot(jnp.concatenate(W, axis=0), ones)
    A = jnp.zeros((C, C), F32)
    for d in range(c):
        A = jnp.where(delta == d, R[d * C:(d + 1) * C], A)
    A = jnp.where(same_diag_block, A, 0.0)

    for li, P in enumerate(_hgrn_levels()):
        half = P // 2
        if half % 8 == 0:
            src, expo = [], []
            for b in range(C // P):
                lo, mid, hi = b * P, b * P + half, (b + 1) * P
                g_mid = G[mid - 1:mid, :]
                src += [k[lo:mid], qs[mid:hi]]
                expo += [g_mid - G[lo:mid], G[mid:hi] - g_mid]
            Z = jnp.concatenate(src, axis=0) * jnp.exp2(jnp.concatenate(expo, axis=0))
        else:
            Ge = jnp.concatenate(
                [jnp.broadcast_to(G[b * P + half - 1:b * P + half, :], (P, HEAD_DIM))
                 for b in range(C // P)], axis=0)
            second = (tok % P) >= half
            Z = jnp.where(second, qs, k) * jnp.exp2(jnp.where(second, 1.0, -1.0) * (G - Ge))
        Z = Z.astype(BF16)
        A = A + mask_ref[li] * _dot_nt(Z, Z)

    o = _dot(A.astype(BF16), v) + _dot_nt((qs * jnp.exp2(G)).astype(BF16), ST.astype(BF16))
    g_last = G[C - 1:C, :]
    k_dec = (k * jnp.exp2(g_last - G)).astype(BF16)
    ST_new = ST * jnp.exp2(g_last) + _dot_tn(v, k_dec)
    return o, ST_new


def _hgrn_kernel(q_ref, lf_ref, k_ref, v_ref, sg_ref, gain_ref, mask_ref, o_ref, st_ref,
                 *, n_chunks, n_heads):
    C, c = HGRN_CHUNK, HGRN_DIAG
    row = lax.broadcasted_iota(jnp.int32, (C, C), 0)
    col = lax.broadcasted_iota(jnp.int32, (C, C), 1)
    tri = (col <= row).astype(BF16)
    ones = jnp.ones((HEAD_DIM, C), BF16)
    delta = row - col
    same_diag_block = (row // c) == (col // c)
    tok = lax.broadcasted_iota(jnp.int32, (C, 1), 0)
    gain = gain_ref[...]
    st_ref[...] = jnp.zeros_like(st_ref)

    def chunk(ci, carry):
        rows = pl.ds(pl.multiple_of(ci * C, C), C)
        for h in range(n_heads):
            sl = slice(h * HEAD_DIM, (h + 1) * HEAD_DIM)
            o, S_new = _hgrn_chunk(q_ref[rows, sl], lf_ref[rows, sl], k_ref[rows, sl],
                                   v_ref[rows, sl], st_ref[h], tri, ones, delta,
                                   same_diag_block, tok, mask_ref)
            st_ref[h] = S_new
            ms = jnp.mean(o * o, axis=-1, keepdims=True)
            on = o * lax.rsqrt(ms + EPS) * gain
            o_ref[rows, sl] = (on * sg_ref[rows, sl].astype(F32)).astype(BF16)
        return carry

    lax.fori_loop(0, n_chunks, chunk, 0, unroll=4)


def _hgrn(q, lf, k, v, sg, out_gain, seq):
    M, D = q.shape
    hb = min(HGRN_HEADS_PER_STEP, D // HEAD_DIM)
    masks = _hgrn_masks()
    blk = lambda: pl.BlockSpec((seq, hb * HEAD_DIM), lambda b, h: (b, h))
    return pl.pallas_call(
        functools.partial(_hgrn_kernel, n_chunks=seq // HGRN_CHUNK, n_heads=hb),
        grid=(M // seq, D // (hb * HEAD_DIM)),
        in_specs=[blk(), blk(), blk(), blk(), blk(),
                  pl.BlockSpec((1, HEAD_DIM), lambda b, h: (0, 0)),
                  pl.BlockSpec(masks.shape, lambda b, h: (0, 0, 0))],
        out_specs=blk(),
        out_shape=jax.ShapeDtypeStruct((M, D), BF16),
        scratch_shapes=[pltpu.VMEM((hb, HEAD_DIM, HEAD_DIM), F32)],
        compiler_params=_params(("parallel", "parallel")),
        name="hgrn",
    )(q, lf, k, v, sg, out_gain, masks)


def _oproj_kernel(a_ref, w_ref, x_ref, mod_ref, o_ref, *, gate_row):
    gate = mod_ref[0, gate_row:gate_row + 1, :]
    o_ref[...] = x_ref[...] + (1 + gate) * _dot(a_ref[...], w_ref[...])


def _oproj(a, w, lead, x, mod, gate_row, seq):
    M, K = a.shape
    D = w.shape[-1]
    tm = _pick(seq, 1024, 8)
    tn = _pick(D, 1024)
    per_b = seq // tm
    return pl.pallas_call(
        functools.partial(_oproj_kernel, gate_row=gate_row),
        grid=(M // tm, D // tn),
        in_specs=[
            pl.BlockSpec((tm, K), lambda i, j: (i, 0)),
            _wspec(lead, (K, tn), lambda i, j: (0, j)),
            pl.BlockSpec((tm, tn), lambda i, j: (i, j)),
            pl.BlockSpec((1, 9, tn), lambda i, j: (i // per_b, 0, j)),
        ],
        out_specs=pl.BlockSpec((tm, tn), lambda i, j: (i, j)),
        out_shape=jax.ShapeDtypeStruct((M, D), F32),
        compiler_params=_params(("parallel", "arbitrary")),
        name="oproj",
    )(a, w, x, mod)


def _attn_kernel(*refs, seq):
    n_g = len(ATTN_GROUPS)
    qkv = refs[:3 * n_g]
    out_ref = refs[3 * n_g]
    s_scr, p_scr, o_scr, m_scr, l_scr = refs[3 * n_g + 1:]
    T = ATTN_BLOCK
    scale = HEAD_DIM ** -0.5
    row = lax.broadcasted_iota(jnp.int32, (T, T), 0)
    col = lax.broadcasted_iota(jnp.int32, (T, T), 1)
    mask_c = col <= row
    mask_p = col >= row

    for gi, (_, dil) in enumerate(ATTN_GROUPS):
        q_ref, k_ref, v_ref = qkv[3 * gi:3 * gi + 3]
        blocks = [(r, n) for r in range(dil) for n in range(seq // dil // T)]

        def keys(n):
            return slice(max(n - 1, 0) * T, (n + 1) * T)

        def tokens(r, n):
            return pl.ds(n * T * dil + r, T, stride=dil) if dil > 1 else pl.ds(n * T, T)

        for bi, (r, n) in enumerate(blocks):
            s = _dot_nt(q_ref[0, 0, r, n * T:(n + 1) * T, :], k_ref[0, 0, r, keys(n), :]) * scale
            s_c = jnp.where(mask_c, s[:, -T:], MASK_VALUE)
            m = jnp.max(s_c, axis=-1, keepdims=True)
            s_scr[bi, :, T:] = s_c
            if n > 0:
                s_p = jnp.where(mask_p, s[:, :T], MASK_VALUE)
                m = jnp.maximum(m, jnp.max(s_p, axis=-1, keepdims=True))
                s_scr[bi, :, :T] = s_p
            m_scr[gi, tokens(r, n), :] = jnp.broadcast_to(m, (T, HEAD_DIM))
        for bi, (r, n) in enumerate(blocks):
            m = m_scr[gi, tokens(r, n), :]
            p_c = jnp.where(mask_c, jnp.exp(s_scr[bi, :, T:] - m), 0.0)
            l = jnp.sum(p_c, axis=-1, keepdims=True)
            p_scr[bi, :, T:] = p_c.astype(BF16)
            if n > 0:
                p_p = jnp.where(mask_p, jnp.exp(s_scr[bi, :, :T] - m), 0.0)
                l = l + jnp.sum(p_p, axis=-1, keepdims=True)
                p_scr[bi, :, :T] = p_p.astype(BF16)
            l_scr[gi, tokens(r, n), :] = jnp.broadcast_to(l, (T, HEAD_DIM))
        for bi, (r, n) in enumerate(blocks):
            p = p_scr[bi] if n > 0 else p_scr[bi, :, T:]
            o_scr[gi, tokens(r, n), :] = _dot(p, v_ref[0, 0, r, keys(n), :])

    for t in range(seq // T):
        rows = pl.ds(t * T, T)
        ms = [m_scr[gi, rows, :] for gi in range(n_g)]
        mx = functools.reduce(jnp.maximum, ms)
        ws = [jnp.exp(m - mx) for m in ms]
        num = sum(w * o_scr[gi, rows, :] for gi, w in enumerate(ws))
        den = sum(w * l_scr[gi, rows, :] for gi, w in enumerate(ws))
        out_ref[rows, :] = (num / den).astype(BF16)


def _attention(qs, ks, vs, batch, seq, n_heads):
    n_g = len(ATTN_GROUPS)
    in_specs, args = [], []
    for gi, (_, dil) in enumerate(ATTN_GROUPS):
        spec = pl.BlockSpec((1, 1, dil, seq // dil, HEAD_DIM), lambda b, h: (b, h, 0, 0, 0))
        in_specs += [spec, spec, spec]
        args += [qs[gi], ks[gi], vs[gi]]
    return pl.pallas_call(
        functools.partial(_attn_kernel, seq=seq),
        grid=(batch, n_heads),
        in_specs=in_specs,
        out_specs=pl.BlockSpec((seq, HEAD_DIM), lambda b, h: (b, h)),
        out_shape=jax.ShapeDtypeStruct((batch * seq, n_heads * HEAD_DIM), BF16),
        scratch_shapes=[pltpu.VMEM((seq // ATTN_BLOCK, ATTN_BLOCK, 2 * ATTN_BLOCK), F32),
                        pltpu.VMEM((seq // ATTN_BLOCK, ATTN_BLOCK, 2 * ATTN_BLOCK), BF16)]
        + [pltpu.VMEM((n_g, seq, HEAD_DIM), F32)] * 3,
        compiler_params=_params(("parallel", "parallel")),
        name="attn",
    )(*args)


def kernel(x, c, norm_g, w_ada, b_ada, w_ffn_in, w_ffn_out, hgrn_w_in, hgrn_w_out, hgrn_lb_logits, hgrn_out_gain, kv_norm_g, kv_w_ada, kv_b_ada, w_kv, k_gain, attn_w_q, attn_q_gain, attn_w_o):
    B, S, D = x.shape
    depth = w_ada.shape[0]
    n_a = hgrn_w_in.shape[0]
    n_groups = k_gain.shape[0]
    width = attn_w_o.shape[1]
    n_heads = width // HEAD_DIM
    GW = n_groups * width
    M = B * S
    assert n_groups == len(ATTN_GROUPS)
    assert all(win // dil == ATTN_BLOCK and S % (dil * ATTN_BLOCK) == 0 for win, dil in ATTN_GROUPS)

    mods = _ada(c, w_ada, b_ada).reshape(depth, B, 9, D)
    kv_mod = _ada(c, kv_w_ada[None], kv_b_ada[None]).reshape(B, 2, D)

    p = jax.nn.softmax(hgrn_lb_logits.astype(F32), axis=0)
    lb = jnp.cumsum(p, axis=0) - p[0]
    lb_floor = jnp.maximum(lb, LB_FLOOR)
    one_m_lb = 1 - lb
    w_ffn_in_b, w_ffn_out_b = w_ffn_in.astype(BF16), w_ffn_out.astype(BF16)
    hgrn_w_in_b, hgrn_w_out_b = hgrn_w_in.astype(BF16), hgrn_w_out.astype(BF16)
    attn_w_q_b, attn_w_o_b = attn_w_q.astype(BF16), attn_w_o.astype(BF16)
    w_kv_b = w_kv.astype(BF16)

    tile = lambda tn: pl.BlockSpec((_pick(S, 512, 8), tn), lambda i, j: (i, j))

    xf = x.reshape(M, D)
    ks = vs = None
    for l in range(depth):
        mod = mods[l]
        xf = _ffn(xf, norm_g[l, 0][None], mod, w_ffn_in_b, w_ffn_out_b, (l, 0), 0, S)
        if l < n_a:
            tn = _pick(D, 512)
            vec = pl.BlockSpec((1, tn), lambda i, j: (0, j))
            q, lf, k, v, sg = _proj(
                xf, norm_g[l, 1][None], mod, hgrn_w_in_b, (l,), [D, 0, 3 * D, 2 * D], D, tn,
                _pick(S, 512, 8), S, 3, 4,
                [lb_floor[l][None], one_m_lb[l][None]], [vec, vec],
                [tile(tn)] * 5,
                [jax.ShapeDtypeStruct((M, D), dt) for dt in (F32, F32, F32, BF16, BF16)],
                _hgrn_epilogue, "hgrn_proj")
            a = _hgrn(q, lf, k, v, sg, hgrn_out_gain[l][None], S)
            xf = _oproj(a, hgrn_w_out_b, (l,), xf, mod, 5, S)
        else:
            jl = l - n_a
            (qs,) = _group_proj(xf, norm_g[l, 1][None], mod, attn_w_q_b, (jl,), [0],
                                attn_q_gain[jl], B, S, n_heads, 3, 4, "q_proj")
            a = _attention(qs, ks, vs, B, S, n_heads)
            xf = _oproj(a, attn_w_o_b, (jl,), xf, mod, 5, S)
        xf = _ffn(xf, norm_g[l, 2][None], mod, w_ffn_in_b, w_ffn_out_b, (l, 1), 2, S)
        if l == n_a - 1:
            ks, vs = _group_proj(xf, kv_norm_g[None], kv_mod, w_kv_b, (), [0, GW], k_gain,
                                 B, S, n_heads, 0, 1, "kv_proj")
    return xf.reshape(B, S, D)
```

```python
import functools

import numpy as np
import jax
import jax.numpy as jnp
from jax import lax
from jax.experimental import pallas as pl
from jax.experimental.pallas import tpu as pltpu

F32 = jnp.float32
BF16 = jnp.bfloat16

EPS = 1e-6
MASK_VALUE = -1e30
LB_FLOOR = 1e-30
LOG2_E = 1.4426950408889634
HEAD_DIM = 128
ATTN_GROUPS = ((128, 1), (512, 4), (2048, 16))
ATTN_BLOCK = 128
HGRN_CHUNK = 128
HGRN_DIAG = 2
HGRN_HEADS_PER_STEP = 4

VMEM_LIMIT_BYTES = 56 * 1024 * 1024


def _pick(n, target, mult=128):
    if n <= target:
        return n
    d = (target // mult) * mult
    while d >= mult:
        if n % d == 0:
            return d
        d -= mult
    raise ValueError(f"no tile for {n}")


def _params(sem):
    return pltpu.CompilerParams(dimension_semantics=sem, vmem_limit_bytes=VMEM_LIMIT_BYTES)


def _silu(x):
    return x * (0.5 + 0.5 * jnp.tanh(0.5 * x))


def _dot(a, b):
    return jnp.dot(a, b, preferred_element_type=F32)


def _dot_nt(a, b):
    return lax.dot_general(a, b, (((1,), (1,)), ((), ())), preferred_element_type=F32)


NORM_ROWS = 32


def _norm_mod_into(hn_ref, x_ref, gain, shift, scale):
    gs = gain * (1 + scale)
    for r in range(0, x_ref.shape[0], NORM_ROWS):
        x = x_ref[r:r + NORM_ROWS, :]
        ms = jnp.mean(x * x, axis=-1, keepdims=True)
        hn_ref[r:r + NORM_ROWS, :] = (x * lax.rsqrt(ms + EPS) * gs + shift).astype(BF16)


def _head_norm(a, gain):
    outs = []
    for h in range(a.shape[1] // HEAD_DIM):
        ah = a[:, h * HEAD_DIM:(h + 1) * HEAD_DIM]
        ms = jnp.mean(ah * ah, axis=-1, keepdims=True)
        outs.append(ah * lax.rsqrt(ms + EPS) * gain)
    return outs[0] if len(outs) == 1 else jnp.concatenate(outs, axis=1)


def _ada_kernel(c_ref, w_ref, b_ref, o_ref):
    sc = _silu(c_ref[...]).astype(BF16)
    o_ref[0] = _dot(sc, w_ref[0].astype(BF16)) + b_ref[0]


def _ada(c, w, b):
    L, D, N = w.shape
    B = c.shape[0]
    tn = _pick(N, 1024)
    return pl.pallas_call(
        _ada_kernel,
        grid=(L, N // tn),
        in_specs=[
            pl.BlockSpec((B, D), lambda l, j: (0, 0)),
            pl.BlockSpec((1, D, tn), lambda l, j: (l, 0, j)),
            pl.BlockSpec((1, 1, tn), lambda l, j: (l, 0, j)),
        ],
        out_specs=pl.BlockSpec((1, B, tn), lambda l, j: (l, 0, j)),
        out_shape=jax.ShapeDtypeStruct((L, B, N), F32),
        compiler_params=_params(("parallel", "parallel")),
        name="ada",
    )(c, w, b.reshape(L, 1, N))


def _ffn_kernel(x_ref, g_ref, mod_ref, w_in_ref, w_out_ref, o_ref,
                hn_ref, wg_buf, wu_buf, wo_buf, sems, *, lead, sub, nj, tf, n_tiles):
    i = pl.program_id(0)

    def copies(j, slot):
        return (
            pltpu.make_async_copy(w_in_ref.at[lead + (j, 0)], wg_buf.at[slot], sems.at[0, slot]),
            pltpu.make_async_copy(w_in_ref.at[lead + (j, 1)], wu_buf.at[slot], sems.at[1, slot]),
            pltpu.make_async_copy(w_out_ref.at[lead + (pl.ds(j * tf, tf), slice(None))],
                                  wo_buf.at[slot], sems.at[2, slot]),
        )

    def start(j, slot):
        for c in copies(j, slot):
            c.start()

    @pl.when(i == 0)
    def _():
        start(0, 0)

    _norm_mod_into(hn_ref, x_ref, g_ref[...], mod_ref[0, 3 * sub:3 * sub + 1, :],
                   mod_ref[0, 3 * sub + 1:3 * sub + 2, :])
    hn = hn_ref[...]
    first = (i * nj) % 2 if nj % 2 else 0
    for j in range(nj):
        slot = (first + j) % 2
        if j + 1 < nj:
            start(j + 1, 1 - slot)
        else:
            @pl.when(i + 1 < n_tiles)
            def _():
                start(0, 1 - slot)
        for c in copies(j, slot):
            c.wait()
        hg = _dot(hn, wg_buf[slot])
        hu = _dot(hn, wu_buf[slot])
        act = (_silu(hg) * hu).astype(BF16)
        y = _dot(act, wo_buf[slot])
        if j == 0:
            o_ref[...] = y
        else:
            o_ref[...] += y
    gate = mod_ref[0, 3 * sub + 2:3 * sub + 3, :]
    o_ref[...] = x_ref[...] + (0.5 * (1 + gate)) * o_ref[...]


def _wspec(lead, block, index_map):
    return pl.BlockSpec((None,) * len(lead) + block, lambda i, j: lead + index_map(i, j))


def _ffn(x, gain, mod, w_in, w_out, lead, sub, seq):
    M, D = x.shape
    nj, tf = w_in.shape[-4], w_in.shape[-1]
    tm = _pick(seq, 512, 8)
    per_b = seq // tm
    return pl.pallas_call(
        functools.partial(_ffn_kernel, lead=lead, sub=sub, nj=nj, tf=tf, n_tiles=M // tm),
        grid=(M // tm,),
        in_specs=[
            pl.BlockSpec((tm, D), lambda i: (i, 0)),
            pl.BlockSpec((1, D), lambda i: (0, 0)),
            pl.BlockSpec((1, 9, D), lambda i: (i // per_b, 0, 0)),
            pl.BlockSpec(memory_space=pl.ANY),
            pl.BlockSpec(memory_space=pl.ANY),
        ],
        out_specs=pl.BlockSpec((tm, D), lambda i: (i, 0)),
        out_shape=jax.ShapeDtypeStruct((M, D), F32),
        scratch_shapes=[pltpu.VMEM((tm, D), BF16),
                        pltpu.VMEM((2, D, tf), BF16), pltpu.VMEM((2, D, tf), BF16),
                        pltpu.VMEM((2, tf, D), BF16), pltpu.SemaphoreType.DMA((3, 2))],
        compiler_params=_params(("arbitrary",)),
        name="ffn",
    )(x, gain, mod, w_in, w_out)


def _proj_kernel(*refs, n_w, n_extra, n_out, shift_row, scale_row, epilogue):
    x_ref, g_ref, mod_ref = refs[:3]
    w_refs = refs[3:3 + n_w]
    extra_refs = refs[3 + n_w:3 + n_w + n_extra]
    out_refs = refs[3 + n_w + n_extra:3 + n_w + n_extra + n_out]
    hn_ref = refs[3 + n_w + n_extra + n_out]
    scratch_refs = refs[4 + n_w + n_extra + n_out:]

    @pl.when(pl.program_id(1) == 0)
    def _():
        _norm_mod_into(hn_ref, x_ref, g_ref[...], mod_ref[0, shift_row:shift_row + 1, :],
                       mod_ref[0, scale_row:scale_row + 1, :])

    hn = hn_ref[...]
    accs = [_dot(hn, w[...]) for w in w_refs]
    epilogue(accs, extra_refs, out_refs, scratch_refs)


def _proj(x, gain, mod, w, lead, col_offsets, n_cols, tn, tm, seq, shift_row, scale_row,
          extras, extra_specs, out_specs, out_shapes, epilogue, name, scratch=()):
    M, D = x.shape
    per_b = seq // tm
    n_mod = mod.shape[1]
    w_specs = [
        _wspec(lead, (D, tn), functools.partial(lambda i, j, o: (0, o + j), o=off // tn))
        for off in col_offsets
    ]
    return pl.pallas_call(
        functools.partial(_proj_kernel, n_w=len(col_offsets), n_extra=len(extras),
                          n_out=len(out_specs), shift_row=shift_row, scale_row=scale_row,
                          epilogue=epilogue),
        grid=(M // tm, n_cols // tn),
        in_specs=[
            pl.BlockSpec((tm, D), lambda i, j: (i, 0)),
            pl.BlockSpec((1, D), lambda i, j: (0, 0)),
            pl.BlockSpec((1, n_mod, D), lambda i, j: (i // per_b, 0, 0)),
        ] + w_specs + extra_specs,
        out_specs=out_specs,
        out_shape=out_shapes,
        scratch_shapes=[pltpu.VMEM((tm, D), BF16), *scratch],
        compiler_params=_params(("parallel", "arbitrary")),
        name=name,
    )(x, gain, mod, *([w] * len(col_offsets)), *extras)


def _hgrn_epilogue(accs, extra_refs, out_refs, scratch_refs):
    ff, q, g, v = accs
    lb_floor, one_m_lb = (r[...] for r in extra_refs)
    q_ref, lf_ref, k_ref, v_ref, sg_ref = out_refs
    e = jnp.exp(-jnp.abs(ff))
    r = 1.0 / (1.0 + e)
    pos = ff >= 0
    lf_ref[...] = jnp.log(lb_floor + one_m_lb * jnp.where(pos, r, e * r)) * LOG2_E
    k_ref[...] = one_m_lb * jnp.where(pos, e * r, r)
    q_ref[...] = _silu(q)
    sg_ref[...] = _silu(g).astype(BF16)
    v_ref[...] = v.astype(BF16)


def _store_dilated(a, scr_ref, out_ref, dil):
    rows, width = a.shape
    for h in range(width // HEAD_DIM):
        ah = a[:, h * HEAD_DIM:(h + 1) * HEAD_DIM]
        if dil == 1:
            out_ref[0, h, 0] = ah.astype(BF16)
        else:
            scr_ref[h] = ah
            for r in range(dil):
                out_ref[0, h, r] = scr_ref[h, pl.ds(r, rows // dil, stride=dil), :].astype(BF16)


def _group_proj_kernel(x_ref, g_ref, mod_ref, gain_ref, w_ref, *refs, lead, col_offsets, width,
                       shift_row, scale_row, n_tiles):
    n_g, n_w = len(ATTN_GROUPS), len(col_offsets)
    out_refs = refs[:n_w * n_g]
    hn_ref, w_buf, scr_ref, sems = refs[n_w * n_g:]
    i = pl.program_id(0)
    order = sorted(range(n_g), key=lambda gi: -ATTN_GROUPS[gi][1])

    def copies(c, slot):
        return [pltpu.make_async_copy(
            w_ref.at[lead + (slice(None), pl.ds(off + order[c] * width, width))],
            w_buf.at[slot, k], sems.at[k, slot]) for k, off in enumerate(col_offsets)]

    def start(c, slot):
        for cp in copies(c, slot):
            cp.start()

    @pl.when(i == 0)
    def _():
        start(0, 0)

    _norm_mod_into(hn_ref, x_ref, g_ref[...], mod_ref[0, shift_row:shift_row + 1, :],
                   mod_ref[0, scale_row:scale_row + 1, :])
    hn = hn_ref[...]
    first = (i * n_g) % 2 if n_g % 2 else 0
    for c, gi in enumerate(order):
        slot = (first + c) % 2
        if c + 1 < n_g:
            start(c + 1, 1 - slot)
        else:
            @pl.when(i + 1 < n_tiles)
            def _():
                start(0, 1 - slot)
        for cp in copies(c, slot):
            cp.wait()
        dil = ATTN_GROUPS[gi][1]
        for k in range(n_w):
            a = _dot(hn, w_buf[slot, k])
            if k == 0:
                a = _head_norm(a, gain_ref[gi])
            _store_dilated(a, scr_ref.at[k], out_refs[k * n_g + gi], dil)


def _group_proj(x, gain, mod, w, lead, col_offsets, head_gain, batch, seq, n_heads,
                shift_row, scale_row, name):
    M, D = x.shape
    n_g, n_w = len(ATTN_GROUPS), len(col_offsets)
    width = n_heads * HEAD_DIM
    tm = _pick(seq, 512, 8 * max(d for _, d in ATTN_GROUPS))
    per_b = seq // tm
    n_mod = mod.shape[1]
    out_specs, out_shapes = [], []
    for _ in range(n_w):
        for _, dil in ATTN_GROUPS:
            out_specs.append(pl.BlockSpec((1, n_heads, dil, tm // dil, HEAD_DIM),
                                          lambda i: (i // per_b, 0, 0, i % per_b, 0)))
            out_shapes.append(
                jax.ShapeDtypeStruct((batch, n_heads, dil, seq // dil, HEAD_DIM), BF16))
    outs = pl.pallas_call(
        functools.partial(_group_proj_kernel, lead=lead, col_offsets=tuple(col_offsets),
                          width=width, shift_row=shift_row, scale_row=scale_row,
                          n_tiles=M // tm),
        grid=(M // tm,),
        in_specs=[
            pl.BlockSpec((tm, D), lambda i: (i, 0)),
            pl.BlockSpec((1, D), lambda i: (0, 0)),
            pl.BlockSpec((1, n_mod, D), lambda i: (i // per_b, 0, 0)),
            pl.BlockSpec((n_g, 1, HEAD_DIM), lambda i: (0, 0, 0)),
            pl.BlockSpec(memory_space=pl.ANY),
        ],
        out_specs=out_specs,
        out_shape=out_shapes,
        scratch_shapes=[pltpu.VMEM((tm, D), BF16), pltpu.VMEM((2, n_w, D, width), BF16),
                        pltpu.VMEM((n_w, n_heads, tm, HEAD_DIM), F32),
                        pltpu.SemaphoreType.DMA((n_w, 2))],
        compiler_params=_params(("arbitrary",)),
        name=name,
    )(x, gain, mod, head_gain[:, None, :], w)
    return [outs[k * n_g:(k + 1) * n_g] for k in range(n_w)]


def _hgrn_levels():
    P, out = 2 * HGRN_DIAG, []
    while P <= HGRN_CHUNK:
        out.append(P)
        P *= 2
    return out


def _hgrn_masks():
    t = np.arange(HGRN_CHUNK)
    rows = []
    for P in _hgrn_levels():
        same = (t[:, None] // P) == (t[None, :] // P)
        rows.append(same & ((t[:, None] % P) >= P // 2) & ((t[None, :] % P) < P // 2))
    return jnp.asarray(np.stack(rows), F32)


def _dot_tn(a, b):
    return lax.dot_general(a, b, (((0,), (0,)), ((), ())), preferred_element_type=F32)


def _hgrn_chunk(qs, lf, k, v, ST, tri, ones, delta, same_diag_block, tok, mask_ref):
    C, c = HGRN_CHUNK, HGRN_DIAG
    lf_hi = lf.astype(BF16)
    lf_lo = (lf - lf_hi.astype(F32)).astype(BF16)
    G2 = _dot(tri, jnp.concatenate([lf_hi, lf_lo], axis=1))
    G = G2[:, :HEAD_DIM] + G2[:, HEAD_DIM:]

    W = [(qs * k).astype(BF16)]
    for d in range(1, c):
        ks = pltpu.roll(k, d, axis=0)
        Gs = pltpu.roll(G, d, axis=0)
        W.append((qs * ks * jnp.exp2(G - Gs)).astype(BF16))
    R = _dot(jnp.concatenate(W, axis=0), ones)
    A = jnp.zeros((C, C), F32)
    for d in range(c):
        A = jnp.where(delta == d, R[d * C:(d + 1) * C], A)
    A = jnp.where(same_diag_block, A, 0.0)

    for li, P in enumerate(_hgrn_levels()):
        half = P // 2
        if half % 8 == 0:
            src, expo = [], []
            for b in range(C // P):
                lo, mid, hi = b * P, b * P + half, (b + 1) * P
                g_mid = G[mid - 1:mid, :]
                src += [k[lo:mid], qs[mid:hi]]
                expo += [g_mid - G[lo:mid], G[mid:hi] - g_mid]
            Z = jnp.concatenate(src, axis=0) * jnp.exp2(jnp.concatenate(expo, axis=0))
        else:
            Ge = jnp.concatenate(
                [jnp.broadcast_to(G[b * P + half - 1:b * P + half, :], (P, HEAD_DIM))
                 for b in range(C // P)], axis=0)
            second = (tok % P) >= half
            Z = jnp.where(second, qs, k) * jnp.exp2(jnp.where(second, 1.0, -1.0) * (G - Ge))
        Z = Z.astype(BF16)
        A = A + mask_ref[li] * _dot_nt(Z, Z)

    o = _dot(A.astype(BF16), v) + _dot_nt((qs * jnp.exp2(G)).astype(BF16), ST.astype(BF16))
    g_last = G[C - 1:C, :]
    k_dec = (k * jnp.exp2(g_last - G)).astype(BF16)
    ST_new = ST * jnp.exp2(g_last) + _dot_tn(v, k_dec)
    return o, ST_new


def _hgrn_kernel(q_ref, lf_ref, k_ref, v_ref, sg_ref, gain_ref, mask_ref, o_ref, st_ref,
                 *, n_chunks, n_heads):
    C, c = HGRN_CHUNK, HGRN_DIAG
    row = lax.broadcasted_iota(jnp.int32, (C, C), 0)
    col = lax.broadcasted_iota(jnp.int32, (C, C), 1)
    tri = (col <= row).astype(BF16)
    ones = jnp.ones((HEAD_DIM, C), BF16)
    delta = row - col
    same_diag_block = (row // c) == (col // c)
    tok = lax.broadcasted_iota(jnp.int32, (C, 1), 0)
    gain = gain_ref[...]
    st_ref[...] = jnp.zeros_like(st_ref)

    def chunk(ci, carry):
        rows = pl.ds(pl.multiple_of(ci * C, C), C)
        for h in range(n_heads):
            sl = slice(h * HEAD_DIM, (h + 1) * HEAD_DIM)
            o, S_new = _hgrn_chunk(q_ref[rows, sl], lf_ref[rows, sl], k_ref[rows, sl],
                                   v_ref[rows, sl], st_ref[h], tri, ones, delta,
                                   same_diag_block, tok, mask_ref)
            st_ref[h] = S_new
            ms = jnp.mean(o * o, axis=-1, keepdims=True)
            on = o * lax.rsqrt(ms + EPS) * gain
            o_ref[rows, sl] = (on * sg_ref[rows, sl].astype(F32)).astype(BF16)
        return carry

    lax.fori_loop(0, n_chunks, chunk, 0, unroll=4)


def _hgrn(q, lf, k, v, sg, out_gain, seq):
    M, D = q.shape
    hb = min(HGRN_HEADS_PER_STEP, D // HEAD_DIM)
    masks = _hgrn_masks()
    blk = lambda: pl.BlockSpec((seq, hb * HEAD_DIM), lambda b, h: (b, h))
    return pl.pallas_call(
        functools.partial(_hgrn_kernel, n_chunks=seq // HGRN_CHUNK, n_heads=hb),
        grid=(M // seq, D // (hb * HEAD_DIM)),
        in_specs=[blk(), blk(), blk(), blk(), blk(),
                  pl.BlockSpec((1, HEAD_DIM), lambda b, h: (0, 0)),
                  pl.BlockSpec(masks.shape, lambda b, h: (0, 0, 0))],
        out_specs=blk(),
        out_shape=jax.ShapeDtypeStruct((M, D), BF16),
        scratch_shapes=[pltpu.VMEM((hb, HEAD_DIM, HEAD_DIM), F32)],
        compiler_params=_params(("parallel", "parallel")),
        name="hgrn",
    )(q, lf, k, v, sg, out_gain, masks)


def _oproj_kernel(a_ref, w_ref, x_ref, mod_ref, o_ref, *, gate_row):
    gate = mod_ref[0, gate_row:gate_row + 1, :]
    o_ref[...] = x_ref[...] + (1 + gate) * _dot(a_ref[...], w_ref[...])


def _oproj(a, w, lead, x, mod, gate_row, seq):
    M, K = a.shape
    D = w.shape[-1]
    tm = _pick(seq, 1024, 8)
    tn = _pick(D, 1024)
    per_b = seq // tm
    return pl.pallas_call(
        functools.partial(_oproj_kernel, gate_row=gate_row),
        grid=(M // tm, D // tn),
        in_specs=[
            pl.BlockSpec((tm, K), lambda i, j: (i, 0)),
            _wspec(lead, (K, tn), lambda i, j: (0, j)),
            pl.BlockSpec((tm, tn), lambda i, j: (i, j)),
            pl.BlockSpec((1, 9, tn), lambda i, j: (i // per_b, 0, j)),
        ],
        out_specs=pl.BlockSpec((tm, tn), lambda i, j: (i, j)),
        out_shape=jax.ShapeDtypeStruct((M, D), F32),
        compiler_params=_params(("parallel", "arbitrary")),
        name="oproj",
    )(a, w, x, mod)


def _attn_kernel(*refs, seq):
    n_g = len(ATTN_GROUPS)
    qkv = refs[:3 * n_g]
    out_ref = refs[3 * n_g]
    s_scr, p_scr, o_scr, m_scr, l_scr = refs[3 * n_g + 1:]
    T = ATTN_BLOCK
    scale = HEAD_DIM ** -0.5
    row = lax.broadcasted_iota(jnp.int32, (T, T), 0)
    col = lax.broadcasted_iota(jnp.int32, (T, T), 1)
    mask_c = col <= row
    mask_p = col >= row

    for gi, (_, dil) in enumerate(ATTN_GROUPS):
        q_ref, k_ref, v_ref = qkv[3 * gi:3 * gi + 3]
        blocks = [(r, n) for r in range(dil) for n in range(seq // dil // T)]

        def keys(n):
            return slice(max(n - 1, 0) * T, (n + 1) * T)

        def tokens(r, n):
            return pl.ds(n * T * dil + r, T, stride=dil) if dil > 1 else pl.ds(n * T, T)

        for bi, (r, n) in enumerate(blocks):
            s = _dot_nt(q_ref[0, 0, r, n * T:(n + 1) * T, :], k_ref[0, 0, r, keys(n), :]) * scale
            s_c = jnp.where(mask_c, s[:, -T:], MASK_VALUE)
            m = jnp.max(s_c, axis=-1, keepdims=True)
            s_scr[bi, :, T:] = s_c
            if n > 0:
                s_p = jnp.where(mask_p, s[:, :T], MASK_VALUE)
                m = jnp.maximum(m, jnp.max(s_p, axis=-1, keepdims=True))
                s_scr[bi, :, :T] = s_p
            m_scr[gi, tokens(r, n), :] = jnp.broadcast_to(m, (T, HEAD_DIM))
        for bi, (r, n) in enumerate(blocks):
            m = m_scr[gi, tokens(r, n), :]
            p_c = jnp.where(mask_c, jnp.exp(s_scr[bi, :, T:] - m), 0.0)
            l = jnp.sum(p_c, axis=-1, keepdims=True)
            p_scr[bi, :, T:] = p_c.astype(BF16)
            if n > 0:
                p_p = jnp.where(mask_p, jnp.exp(s_scr[bi, :, :T] - m), 0.0)
                l = l + jnp.sum(p_p, axis=-1, keepdims=True)
                p_scr[bi, :, :T] = p_p.astype(BF16)
            l_scr[gi, tokens(r, n), :] = jnp.broadcast_to(l, (T, HEAD_DIM))
        for bi, (r, n) in enumerate(blocks):
            p = p_scr[bi] if n > 0 else p_scr[bi, :, T:]
            o_scr[gi, tokens(r, n), :] = _dot(p, v_ref[0, 0, r, keys(n), :])

    for t in range(seq // T):
        rows = pl.ds(t * T, T)
        ms = [m_scr[gi, rows, :] for gi in range(n_g)]
        mx = functools.reduce(jnp.maximum, ms)
        ws = [jnp.exp(m - mx) for m in ms]
        num = sum(w * o_scr[gi, rows, :] for gi, w in enumerate(ws))
        den = sum(w * l_scr[gi, rows, :] for gi, w in enumerate(ws))
        out_ref[rows, :] = (num / den).astype(BF16)


def _attention(qs, ks, vs, batch, seq, n_heads):
    n_g = len(ATTN_GROUPS)
    in_specs, args = [], []
    for gi, (_, dil) in enumerate(ATTN_GROUPS):
        spec = pl.BlockSpec((1, 1, dil, seq // dil, HEAD_DIM), lambda b, h: (b, h, 0, 0, 0))
        in_specs += [spec, spec, spec]
        args += [qs[gi], ks[gi], vs[gi]]
    return pl.pallas_call(
        functools.partial(_attn_kernel, seq=seq),
        grid=(batch, n_heads),
        in_specs=in_specs,
        out_specs=pl.BlockSpec((seq, HEAD_DIM), lambda b, h: (b, h)),
        out_shape=jax.ShapeDtypeStruct((batch * seq, n_heads * HEAD_DIM), BF16),
        scratch_shapes=[pltpu.VMEM((seq // ATTN_BLOCK, ATTN_BLOCK, 2 * ATTN_BLOCK), F32),
                        pltpu.VMEM((seq // ATTN_BLOCK, ATTN_BLOCK, 2 * ATTN_BLOCK), BF16)]
        + [pltpu.VMEM((n_g, seq, HEAD_DIM), F32)] * 3,
        compiler_params=_params(("parallel", "parallel")),
        name="attn",
    )(*args)


def kernel(x, c, norm_g, w_ada, b_ada, w_ffn_in, w_ffn_out, hgrn_w_in, hgrn_w_out, hgrn_lb_logits, hgrn_out_gain, kv_norm_g, kv_w_ada, kv_b_ada, w_kv, k_gain, attn_w_q, attn_q_gain, attn_w_o):
    B, S, D = x.shape
    depth = w_ada.shape[0]
    n_a = hgrn_w_in.shape[0]
    n_groups = k_gain.shape[0]
    width = attn_w_o.shape[1]
    n_heads = width // HEAD_DIM
    GW = n_groups * width
    M = B * S
    assert n_groups == len(ATTN_GROUPS)
    assert all(win // dil == ATTN_BLOCK and S % (dil * ATTN_BLOCK) == 0 for win, dil in ATTN_GROUPS)

    mods = _ada(c, w_ada, b_ada).reshape(depth, B, 9, D)
    kv_mod = _ada(c, kv_w_ada[None], kv_b_ada[None]).reshape(B, 2, D)

    p = jax.nn.softmax(hgrn_lb_logits.astype(F32), axis=0)
    lb = jnp.cumsum(p, axis=0) - p[0]
    lb_floor = jnp.maximum(lb, LB_FLOOR)
    one_m_lb = 1 - lb
    F = w_ffn_out.shape[-2]
    tf = _pick(F, 512)
    w_ffn_in_b = w_ffn_in.astype(BF16).reshape(depth, 2, D, 2, F // tf, tf).transpose(0, 1, 4, 3, 2, 5)
    w_ffn_out_b = w_ffn_out.astype(BF16)
    hgrn_w_in_b, hgrn_w_out_b = hgrn_w_in.astype(BF16), hgrn_w_out.astype(BF16)
    attn_w_q_b, attn_w_o_b = attn_w_q.astype(BF16), attn_w_o.astype(BF16)
    w_kv_b = w_kv.astype(BF16)

    tile = lambda tn: pl.BlockSpec((_pick(S, 512, 8), tn), lambda i, j: (i, j))

    xf = x.reshape(M, D)
    ks = vs = None
    for l in range(depth):
        mod = mods[l]
        xf = _ffn(xf, norm_g[l, 0][None], mod, w_ffn_in_b, w_ffn_out_b, (l, 0), 0, S)
        if l < n_a:
            tn = _pick(D, 512)
            vec = pl.BlockSpec((1, tn), lambda i, j: (0, j))
            q, lf, k, v, sg = _proj(
                xf, norm_g[l, 1][None], mod, hgrn_w_in_b, (l,), [D, 0, 3 * D, 2 * D], D, tn,
                _pick(S, 512, 8), S, 3, 4,
                [lb_floor[l][None], one_m_lb[l][None]], [vec, vec],
                [tile(tn)] * 5,
                [jax.ShapeDtypeStruct((M, D), dt) for dt in (F32, F32, F32, BF16, BF16)],
                _hgrn_epilogue, "hgrn_proj")
            a = _hgrn(q, lf, k, v, sg, hgrn_out_gain[l][None], S)
            xf = _oproj(a, hgrn_w_out_b, (l,), xf, mod, 5, S)
        else:
            jl = l - n_a
            (qs,) = _group_proj(xf, norm_g[l, 1][None], mod, attn_w_q_b, (jl,), [0],
                                attn_q_gain[jl], B, S, n_heads, 3, 4, "q_proj")
            a = _attention(qs, ks, vs, B, S, n_heads)
            xf = _oproj(a, attn_w_o_b, (jl,), xf, mod, 5, S)
        xf = _ffn(xf, norm_g[l, 2][None], mod, w_ffn_in_b, w_ffn_out_b, (l, 1), 2, S)
        if l == n_a - 1:
            ks, vs = _group_proj(xf, kv_norm_g[None], kv_mod, w_kv_b, (), [0, GW], k_gain,
                                 B, S, n_heads, 0, 1, "kv_proj")
    return xf.reshape(B, S, D)
```

```python
import functools

import numpy as np
import jax
import jax.numpy as jnp
from jax import lax
from jax.experimental import pallas as pl
from jax.experimental.pallas import tpu as pltpu

F32 = jnp.float32
BF16 = jnp.bfloat16

EPS = 1e-6
MASK_VALUE = -1e30
LB_FLOOR = 1e-30
LOG2_E = 1.4426950408889634
HEAD_DIM = 128
ATTN_GROUPS = ((128, 1), (512, 4), (2048, 16))
ATTN_BLOCK = 128
HGRN_CHUNK = 128
HGRN_DIAG = 2
HGRN_HEADS_PER_STEP = 4

VMEM_LIMIT_BYTES = 56 * 1024 * 1024


def _pick(n, target, mult=128):
    if n <= target:
        return n
    d = (target // mult) * mult
    while d >= mult:
        if n % d == 0:
            return d
        d -= mult
    raise ValueError(f"no tile for {n}")


def _params(sem):
    return pltpu.CompilerParams(dimension_semantics=sem, vmem_limit_bytes=VMEM_LIMIT_BYTES)


def _silu(x):
    return x * (0.5 + 0.5 * jnp.tanh(0.5 * x))


def _dot(a, b):
    return jnp.dot(a, b, preferred_element_type=F32)


def _dot_nt(a, b):
    return lax.dot_general(a, b, (((1,), (1,)), ((), ())), preferred_element_type=F32)


NORM_ROWS = 32


def _norm_mod_into(hn_ref, x_ref, gain, shift, scale):
    gs = gain * (1 + scale)
    for r in range(0, x_ref.shape[0], NORM_ROWS):
        x = x_ref[r:r + NORM_ROWS, :]
        ms = jnp.mean(x * x, axis=-1, keepdims=True)
        hn_ref[r:r + NORM_ROWS, :] = (x * lax.rsqrt(ms + EPS) * gs + shift).astype(BF16)


def _head_norm(a, gain):
    outs = []
    for h in range(a.shape[1] // HEAD_DIM):
        ah = a[:, h * HEAD_DIM:(h + 1) * HEAD_DIM]
        ms = jnp.mean(ah * ah, axis=-1, keepdims=True)
        outs.append(ah * lax.rsqrt(ms + EPS) * gain)
    return outs[0] if len(outs) == 1 else jnp.concatenate(outs, axis=1)


def _ada_kernel(c_ref, w_ref, b_ref, o_ref):
    sc = _silu(c_ref[...]).astype(BF16)
    o_ref[0] = _dot(sc, w_ref[0].astype(BF16)) + b_ref[0]


def _ada(c, w, b):
    L, D, N = w.shape
    B = c.shape[0]
    tn = _pick(N, 1024)
    return pl.pallas_call(
        _ada_kernel,
        grid=(L, N // tn),
        in_specs=[
            pl.BlockSpec((B, D), lambda l, j: (0, 0)),
            pl.BlockSpec((1, D, tn), lambda l, j: (l, 0, j)),
            pl.BlockSpec((1, 1, tn), lambda l, j: (l, 0, j)),
        ],
        out_specs=pl.BlockSpec((1, B, tn), lambda l, j: (l, 0, j)),
        out_shape=jax.ShapeDtypeStruct((L, B, N), F32),
        compiler_params=_params(("parallel", "parallel")),
        name="ada",
    )(c, w, b.reshape(L, 1, N))


def _ffn_kernel(x_ref, g_ref, mod_ref, w_in_ref, w_out_ref, o_ref,
                hn_ref, wg_buf, wu_buf, wo_buf, sems, *, lead, sub, nj, tf, n_tiles):
    i = pl.program_id(0)
    F = nj * tf

    def copies(j, slot):
        cols = pl.ds(j * tf, tf)
        return (
            pltpu.make_async_copy(w_in_ref.at[lead + (slice(None), cols)], wg_buf.at[slot],
                                  sems.at[0, slot]),
            pltpu.make_async_copy(w_in_ref.at[lead + (slice(None), pl.ds(F + j * tf, tf))],
                                  wu_buf.at[slot], sems.at[1, slot]),
            pltpu.make_async_copy(w_out_ref.at[lead + (cols, slice(None))], wo_buf.at[slot],
                                  sems.at[2, slot]),
        )

    def start(j, slot):
        for c in copies(j, slot):
            c.start()

    @pl.when(i == 0)
    def _():
        start(0, 0)

    _norm_mod_into(hn_ref, x_ref, g_ref[...], mod_ref[0, 3 * sub:3 * sub + 1, :],
                   mod_ref[0, 3 * sub + 1:3 * sub + 2, :])
    hn = hn_ref[...]
    first = (i * nj) % 2 if nj % 2 else 0
    for j in range(nj):
        slot = (first + j) % 2
        if j + 1 < nj:
            start(j + 1, 1 - slot)
        else:
            @pl.when(i + 1 < n_tiles)
            def _():
                start(0, 1 - slot)
        for c in copies(j, slot):
            c.wait()
        hg = _dot(hn, wg_buf[slot])
        hu = _dot(hn, wu_buf[slot])
        act = (_silu(hg) * hu).astype(BF16)
        y = _dot(act, wo_buf[slot])
        if j == 0:
            o_ref[...] = y
        else:
            o_ref[...] += y
    gate = mod_ref[0, 3 * sub + 2:3 * sub + 3, :]
    o_ref[...] = x_ref[...] + (0.5 * (1 + gate)) * o_ref[...]


def _wspec(lead, block, index_map):
    return pl.BlockSpec((None,) * len(lead) + block, lambda i, j: lead + index_map(i, j))


def _ffn(x, gain, mod, w_in, w_out, lead, sub, seq):
    M, D = x.shape
    F = w_out.shape[-2]
    tm = _pick(seq, 512, 8)
    tf = _pick(F, 512)
    nj = F // tf
    per_b = seq // tm
    return pl.pallas_call(
        functools.partial(_ffn_kernel, lead=lead, sub=sub, nj=nj, tf=tf, n_tiles=M // tm),
        grid=(M // tm,),
        in_specs=[
            pl.BlockSpec((tm, D), lambda i: (i, 0)),
            pl.BlockSpec((1, D), lambda i: (0, 0)),
            pl.BlockSpec((1, 9, D), lambda i: (i // per_b, 0, 0)),
            pl.BlockSpec(memory_space=pl.ANY),
            pl.BlockSpec(memory_space=pl.ANY),
        ],
        out_specs=pl.BlockSpec((tm, D), lambda i: (i, 0)),
        out_shape=jax.ShapeDtypeStruct((M, D), F32),
        scratch_shapes=[pltpu.VMEM((tm, D), BF16),
                        pltpu.VMEM((2, D, tf), BF16), pltpu.VMEM((2, D, tf), BF16),
                        pltpu.VMEM((2, tf, D), BF16), pltpu.SemaphoreType.DMA((3, 2))],
        compiler_params=_params(("arbitrary",)),
        name="ffn",
    )(x, gain, mod, w_in, w_out)


def _proj_kernel(*refs, n_w, n_extra, n_out, shift_row, scale_row, epilogue):
    x_ref, g_ref, mod_ref = refs[:3]
    w_refs = refs[3:3 + n_w]
    extra_refs = refs[3 + n_w:3 + n_w + n_extra]
    out_refs = refs[3 + n_w + n_extra:3 + n_w + n_extra + n_out]
    hn_ref = refs[3 + n_w + n_extra + n_out]
    scratch_refs = refs[4 + n_w + n_extra + n_out:]

    @pl.when(pl.program_id(1) == 0)
    def _():
        _norm_mod_into(hn_ref, x_ref, g_ref[...], mod_ref[0, shift_row:shift_row + 1, :],
                       mod_ref[0, scale_row:scale_row + 1, :])

    hn = hn_ref[...]
    accs = [_dot(hn, w[...]) for w in w_refs]
    epilogue(accs, extra_refs, out_refs, scratch_refs)


def _proj(x, gain, mod, w, lead, col_offsets, n_cols, tn, tm, seq, shift_row, scale_row,
          extras, extra_specs, out_specs, out_shapes, epilogue, name, scratch=()):
    M, D = x.shape
    per_b = seq // tm
    n_mod = mod.shape[1]
    w_specs = [
        _wspec(lead, (D, tn), functools.partial(lambda i, j, o: (0, o + j), o=off // tn))
        for off in col_offsets
    ]
    return pl.pallas_call(
        functools.partial(_proj_kernel, n_w=len(col_offsets), n_extra=len(extras),
                          n_out=len(out_specs), shift_row=shift_row, scale_row=scale_row,
                          epilogue=epilogue),
        grid=(M // tm, n_cols // tn),
        in_specs=[
            pl.BlockSpec((tm, D), lambda i, j: (i, 0)),
            pl.BlockSpec((1, D), lambda i, j: (0, 0)),
            pl.BlockSpec((1, n_mod, D), lambda i, j: (i // per_b, 0, 0)),
        ] + w_specs + extra_specs,
        out_specs=out_specs,
        out_shape=out_shapes,
        scratch_shapes=[pltpu.VMEM((tm, D), BF16), *scratch],
        compiler_params=_params(("parallel", "arbitrary")),
        name=name,
    )(x, gain, mod, *([w] * len(col_offsets)), *extras)


def _hgrn_epilogue(accs, extra_refs, out_refs, scratch_refs):
    ff, q, g, v = accs
    lb_floor, one_m_lb = (r[...] for r in extra_refs)
    q_ref, lf_ref, k_ref, v_ref, sg_ref = out_refs
    e = jnp.exp(-jnp.abs(ff))
    r = 1.0 / (1.0 + e)
    pos = ff >= 0
    lf_ref[...] = jnp.log(lb_floor + one_m_lb * jnp.where(pos, r, e * r)) * LOG2_E
    k_ref[...] = one_m_lb * jnp.where(pos, e * r, r)
    q_ref[...] = _silu(q)
    sg_ref[...] = _silu(g).astype(BF16)
    v_ref[...] = v.astype(BF16)


def _store_dilated(a, scr_ref, out_ref, dil):
    rows, width = a.shape
    for h in range(width // HEAD_DIM):
        ah = a[:, h * HEAD_DIM:(h + 1) * HEAD_DIM]
        if dil == 1:
            out_ref[0, h, 0] = ah.astype(BF16)
        else:
            scr_ref[h] = ah
            for r in range(dil):
                out_ref[0, h, r] = scr_ref[h, pl.ds(r, rows // dil, stride=dil), :].astype(BF16)


def _group_proj_kernel(x_ref, g_ref, mod_ref, gain_ref, w_ref, *refs, lead, col_offsets, width,
                       shift_row, scale_row, n_tiles):
    n_g, n_w = len(ATTN_GROUPS), len(col_offsets)
    out_refs = refs[:n_w * n_g]
    hn_ref, w_buf, scr_ref, sems = refs[n_w * n_g:]
    i = pl.program_id(0)
    order = sorted(range(n_g), key=lambda gi: -ATTN_GROUPS[gi][1])

    def copies(c, slot):
        return [pltpu.make_async_copy(
            w_ref.at[lead + (slice(None), pl.ds(off + order[c] * width, width))],
            w_buf.at[slot, k], sems.at[k, slot]) for k, off in enumerate(col_offsets)]

    def start(c, slot):
        for cp in copies(c, slot):
            cp.start()

    @pl.when(i == 0)
    def _():
        start(0, 0)

    _norm_mod_into(hn_ref, x_ref, g_ref[...], mod_ref[0, shift_row:shift_row + 1, :],
                   mod_ref[0, scale_row:scale_row + 1, :])
    hn = hn_ref[...]
    first = (i * n_g) % 2 if n_g % 2 else 0
    for c, gi in enumerate(order):
        slot = (first + c) % 2
        if c + 1 < n_g:
            start(c + 1, 1 - slot)
        else:
            @pl.when(i + 1 < n_tiles)
            def _():
                start(0, 1 - slot)
        for cp in copies(c, slot):
            cp.wait()
        dil = ATTN_GROUPS[gi][1]
        for k in range(n_w):
            a = _dot(hn, w_buf[slot, k])
            if k == 0:
                a = _head_norm(a, gain_ref[gi])
            _store_dilated(a, scr_ref.at[k], out_refs[k * n_g + gi], dil)


def _group_proj(x, gain, mod, w, lead, col_offsets, head_gain, batch, seq, n_heads,
                shift_row, scale_row, name):
    M, D = x.shape
    n_g, n_w = len(ATTN_GROUPS), len(col_offsets)
    width = n_heads * HEAD_DIM
    tm = _pick(seq, 512, 8 * max(d for _, d in ATTN_GROUPS))
    per_b = seq // tm
    n_mod = mod.shape[1]
    out_specs, out_shapes = [], []
    for _ in range(n_w):
        for _, dil in ATTN_GROUPS:
            out_specs.append(pl.BlockSpec((1, n_heads, dil, tm // dil, HEAD_DIM),
                                          lambda i: (i // per_b, 0, 0, i % per_b, 0)))
            out_shapes.append(
                jax.ShapeDtypeStruct((batch, n_heads, dil, seq // dil, HEAD_DIM), BF16))
    outs = pl.pallas_call(
        functools.partial(_group_proj_kernel, lead=lead, col_offsets=tuple(col_offsets),
                          width=width, shift_row=shift_row, scale_row=scale_row,
                          n_tiles=M // tm),
        grid=(M // tm,),
        in_specs=[
            pl.BlockSpec((tm, D), lambda i: (i, 0)),
            pl.BlockSpec((1, D), lambda i: (0, 0)),
            pl.BlockSpec((1, n_mod, D), lambda i: (i // per_b, 0, 0)),
            pl.BlockSpec((n_g, 1, HEAD_DIM), lambda i: (0, 0, 0)),
            pl.BlockSpec(memory_space=pl.ANY),
        ],
        out_specs=out_specs,
        out_shape=out_shapes,
        scratch_shapes=[pltpu.VMEM((tm, D), BF16), pltpu.VMEM((2, n_w, D, width), BF16),
                        pltpu.VMEM((n_w, n_heads, tm, HEAD_DIM), F32),
                        pltpu.SemaphoreType.DMA((n_w, 2))],
        compiler_params=_params(("arbitrary",)),
        name=name,
    )(x, gain, mod, head_gain[:, None, :], w)
    return [outs[k * n_g:(k + 1) * n_g] for k in range(n_w)]


def _hgrn_levels():
    P, out = 2 * HGRN_DIAG, []
    while P <= HGRN_CHUNK:
        out.append(P)
        P *= 2
    return out


def _hgrn_masks():
    t = np.arange(HGRN_CHUNK)
    rows = []
    for P in _hgrn_levels():
        same = (t[:, None] // P) == (t[None, :] // P)
        rows.append(same & ((t[:, None] % P) >= P // 2) & ((t[None, :] % P) < P // 2))
    return jnp.asarray(np.stack(rows), F32)


def _dot_tn(a, b):
    return lax.dot_general(a, b, (((0,), (0,)), ((), ())), preferred_element_type=F32)


def _hgrn_chunk(qs, lf, k, v, ST, tri, ones, delta, same_diag_block, tok, mask_ref):
    C, c = HGRN_CHUNK, HGRN_DIAG
    lf_hi = lf.astype(BF16)
    lf_lo = (lf - lf_hi.astype(F32)).astype(BF16)
    G = _dot(tri, lf_hi) + _dot(tri, lf_lo)

    W = [(qs * k).astype(BF16)]
    for d in range(1, c):
        ks = pltpu.roll(k, d, axis=0)
        Gs = pltpu.roll(G, d, axis=0)
        W.append((qs * ks * jnp.exp2(G - Gs)).astype(BF16))
    R = _dot(jnp.concatenate(W, axis=0), ones)
    A = jnp.zeros((C, C), F32)
    for d in range(c):
        A = jnp.where(delta == d, R[d * C:(d + 1) * C], A)
    A = jnp.where(same_diag_block, A, 0.0)

    for li, P in enumerate(_hgrn_levels()):
        half = P // 2
        if half % 8 == 0:
            src, expo = [], []
            for b in range(C // P):
                lo, mid, hi = b * P, b * P + half, (b + 1) * P
                g_mid = G[mid - 1:mid, :]
                src += [k[lo:mid], qs[mid:hi]]
                expo += [g_mid - G[lo:mid], G[mid:hi] - g_mid]
            Z = jnp.concatenate(src, axis=0) * jnp.exp2(jnp.concatenate(expo, axis=0))
        else:
            Ge = jnp.concatenate(
                [jnp.broadcast_to(G[b * P + half - 1:b * P + half, :], (P, HEAD_DIM))
                 for b in range(C // P)], axis=0)
            second = (tok % P) >= half
            Z = jnp.where(second, qs, k) * jnp.exp2(jnp.where(second, 1.0, -1.0) * (G - Ge))
        Z = Z.astype(BF16)
        A = A + mask_ref[li] * _dot_nt(Z, Z)

    o = _dot(A.astype(BF16), v) + _dot_nt((qs * jnp.exp2(G)).astype(BF16), ST.astype(BF16))
    g_last = G[C - 1:C, :]
    k_dec = (k * jnp.exp2(g_last - G)).astype(BF16)
    ST_new = ST * jnp.exp2(g_last) + _dot_tn(v, k_dec)
    return o, ST_new


def _hgrn_kernel(q_ref, lf_ref, k_ref, v_ref, sg_ref, gain_ref, mask_ref, o_ref, st_ref,
                 *, n_chunks, n_heads):
    C, c = HGRN_CHUNK, HGRN_DIAG
    row = lax.broadcasted_iota(jnp.int32, (C, C), 0)
    col = lax.broadcasted_iota(jnp.int32, (C, C), 1)
    tri = (col <= row).astype(BF16)
    ones = jnp.ones((HEAD_DIM, C), BF16)
    delta = row - col
    same_diag_block = (row // c) == (col // c)
    tok = lax.broadcasted_iota(jnp.int32, (C, 1), 0)
    gain = gain_ref[...]
    st_ref[...] = jnp.zeros_like(st_ref)

    def chunk(ci, carry):
        rows = pl.ds(pl.multiple_of(ci * C, C), C)
        for h in range(n_heads):
            sl = slice(h * HEAD_DIM, (h + 1) * HEAD_DIM)
            o, S_new = _hgrn_chunk(q_ref[rows, sl], lf_ref[rows, sl], k_ref[rows, sl],
                                   v_ref[rows, sl], st_ref[h], tri, ones, delta,
                                   same_diag_block, tok, mask_ref)
            st_ref[h] = S_new
            ms = jnp.mean(o * o, axis=-1, keepdims=True)
            on = o * lax.rsqrt(ms + EPS) * gain
            o_ref[rows, sl] = (on * sg_ref[rows, sl].astype(F32)).astype(BF16)
        return carry

    lax.fori_loop(0, n_chunks, chunk, 0, unroll=4)


def _hgrn(q, lf, k, v, sg, out_gain, seq):
    M, D = q.shape
    hb = min(HGRN_HEADS_PER_STEP, D // HEAD_DIM)
    masks = _hgrn_masks()
    blk = lambda: pl.BlockSpec((seq, hb * HEAD_DIM), lambda b, h: (b, h))
    return pl.pallas_call(
        functools.partial(_hgrn_kernel, n_chunks=seq // HGRN_CHUNK, n_heads=hb),
        grid=(M // seq, D // (hb * HEAD_DIM)),
        in_specs=[blk(), blk(), blk(), blk(), blk(),
                  pl.BlockSpec((1, HEAD_DIM), lambda b, h: (0, 0)),
                  pl.BlockSpec(masks.shape, lambda b, h: (0, 0, 0))],
        out_specs=blk(),
        out_shape=jax.ShapeDtypeStruct((M, D), BF16),
        scratch_shapes=[pltpu.VMEM((hb, HEAD_DIM, HEAD_DIM), F32)],
        compiler_params=_params(("parallel", "parallel")),
        name="hgrn",
    )(q, lf, k, v, sg, out_gain, masks)


def _oproj_kernel(a_ref, w_ref, x_ref, mod_ref, o_ref, *, gate_row):
    gate = mod_ref[0, gate_row:gate_row + 1, :]
    o_ref[...] = x_ref[...] + (1 + gate) * _dot(a_ref[...], w_ref[...])


def _oproj(a, w, lead, x, mod, gate_row, seq):
    M, K = a.shape
    D = w.shape[-1]
    tm = _pick(seq, 1024, 8)
    tn = _pick(D, 1024)
    per_b = seq // tm
    return pl.pallas_call(
        functools.partial(_oproj_kernel, gate_row=gate_row),
        grid=(M // tm, D // tn),
        in_specs=[
            pl.BlockSpec((tm, K), lambda i, j: (i, 0)),
            _wspec(lead, (K, tn), lambda i, j: (0, j)),
            pl.BlockSpec((tm, tn), lambda i, j: (i, j)),
            pl.BlockSpec((1, 9, tn), lambda i, j: (i // per_b, 0, j)),
        ],
        out_specs=pl.BlockSpec((tm, tn), lambda i, j: (i, j)),
        out_shape=jax.ShapeDtypeStruct((M, D), F32),
        compiler_params=_params(("parallel", "arbitrary")),
        name="oproj",
    )(a, w, x, mod)


def _attn_kernel(*refs, seq):
    n_g = len(ATTN_GROUPS)
    qkv = refs[:3 * n_g]
    out_ref = refs[3 * n_g]
    s_scr, p_scr, o_scr, m_scr, l_scr = refs[3 * n_g + 1:]
    T = ATTN_BLOCK
    scale = HEAD_DIM ** -0.5
    row = lax.broadcasted_iota(jnp.int32, (T, T), 0)
    col = lax.broadcasted_iota(jnp.int32, (T, T), 1)
    mask_c = col <= row
    mask_p = col >= row

    for gi, (_, dil) in enumerate(ATTN_GROUPS):
        q_ref, k_ref, v_ref = qkv[3 * gi:3 * gi + 3]
        blocks = [(r, n) for r in range(dil) for n in range(seq // dil // T)]

        def keys(n):
            return slice(max(n - 1, 0) * T, (n + 1) * T)

        def tokens(r, n):
            return pl.ds(n * T * dil + r, T, stride=dil) if dil > 1 else pl.ds(n * T, T)

        for bi, (r, n) in enumerate(blocks):
            s = _dot_nt(q_ref[0, 0, r, n * T:(n + 1) * T, :], k_ref[0, 0, r, keys(n), :]) * scale
            s_c = jnp.where(mask_c, s[:, -T:], MASK_VALUE)
            m = jnp.max(s_c, axis=-1, keepdims=True)
            s_scr[bi, :, T:] = s_c
            if n > 0:
                s_p = jnp.where(mask_p, s[:, :T], MASK_VALUE)
                m = jnp.maximum(m, jnp.max(s_p, axis=-1, keepdims=True))
                s_scr[bi, :, :T] = s_p
            m_scr[gi, tokens(r, n), :] = jnp.broadcast_to(m, (T, HEAD_DIM))
        for bi, (r, n) in enumerate(blocks):
            m = m_scr[gi, tokens(r, n), :]
            p_scr[bi, :, T:] = jnp.where(
                mask_c, jnp.exp(s_scr[bi, :, T:] - m), 0.0).astype(BF16)
            if n > 0:
                p_scr[bi, :, :T] = jnp.where(
                    mask_p, jnp.exp(s_scr[bi, :, :T] - m), 0.0).astype(BF16)
        for bi, (r, n) in enumerate(blocks):
            p = p_scr[bi] if n > 0 else p_scr[bi, :, T:]
            v = v_ref[0, 0, r, keys(n), :]
            o = _dot(p, jnp.concatenate([v, jnp.ones_like(v)], axis=1))
            o_scr[gi, tokens(r, n), :] = o[:, :HEAD_DIM]
            l_scr[gi, tokens(r, n), :] = o[:, HEAD_DIM:]

    for t in range(seq // T):
        rows = pl.ds(t * T, T)
        ms = [m_scr[gi, rows, :] for gi in range(n_g)]
        mx = functools.reduce(jnp.maximum, ms)
        ws = [jnp.exp(m - mx) for m in ms]
        num = sum(w * o_scr[gi, rows, :] for gi, w in enumerate(ws))
        den = sum(w * l_scr[gi, rows, :] for gi, w in enumerate(ws))
        out_ref[rows, :] = (num / den).astype(BF16)


def _attention(qs, ks, vs, batch, seq, n_heads):
    n_g = len(ATTN_GROUPS)
    in_specs, args = [], []
    for gi, (_, dil) in enumerate(ATTN_GROUPS):
        spec = pl.BlockSpec((1, 1, dil, seq // dil, HEAD_DIM), lambda b, h: (b, h, 0, 0, 0))
        in_specs += [spec, spec, spec]
        args += [qs[gi], ks[gi], vs[gi]]
    return pl.pallas_call(
        functools.partial(_attn_kernel, seq=seq),
        grid=(batch, n_heads),
        in_specs=in_specs,
        out_specs=pl.BlockSpec((seq, HEAD_DIM), lambda b, h: (b, h)),
        out_shape=jax.ShapeDtypeStruct((batch * seq, n_heads * HEAD_DIM), BF16),
        scratch_shapes=[pltpu.VMEM((seq // ATTN_BLOCK, ATTN_BLOCK, 2 * ATTN_BLOCK), F32),
                        pltpu.VMEM((seq // ATTN_BLOCK, ATTN_BLOCK, 2 * ATTN_BLOCK), BF16)]
        + [pltpu.VMEM((n_g, seq, HEAD_DIM), F32)] * 3,
        compiler_params=_params(("parallel", "parallel")),
        name="attn",
    )(*args)


def kernel(x, c, norm_g, w_ada, b_ada, w_ffn_in, w_ffn_out, hgrn_w_in, hgrn_w_out, hgrn_lb_logits, hgrn_out_gain, kv_norm_g, kv_w_ada, kv_b_ada, w_kv, k_gain, attn_w_q, attn_q_gain, attn_w_o):
    B, S, D = x.shape
    depth = w_ada.shape[0]
    n_a = hgrn_w_in.shape[0]
    n_groups = k_gain.shape[0]
    width = attn_w_o.shape[1]
    n_heads = width // HEAD_DIM
    GW = n_groups * width
    M = B * S
    assert n_groups == len(ATTN_GROUPS)
    assert all(win // dil == ATTN_BLOCK and S % (dil * ATTN_BLOCK) == 0 for win, dil in ATTN_GROUPS)

    mods = _ada(c, w_ada, b_ada).reshape(depth, B, 9, D)
    kv_mod = _ada(c, kv_w_ada[None], kv_b_ada[None]).reshape(B, 2, D)

    p = jax.nn.softmax(hgrn_lb_logits.astype(F32), axis=0)
    lb = jnp.cumsum(p, axis=0) - p[0]
    lb_floor = jnp.maximum(lb, LB_FLOOR)
    one_m_lb = 1 - lb
    w_ffn_in_b, w_ffn_out_b = w_ffn_in.astype(BF16), w_ffn_out.astype(BF16)
    hgrn_w_in_b, hgrn_w_out_b = hgrn_w_in.astype(BF16), hgrn_w_out.astype(BF16)
    attn_w_q_b, attn_w_o_b = attn_w_q.astype(BF16), attn_w_o.astype(BF16)
    w_kv_b = w_kv.astype(BF16)

    tile = lambda tn: pl.BlockSpec((_pick(S, 512, 8), tn), lambda i, j: (i, j))

    xf = x.reshape(M, D)
    ks = vs = None
    for l in range(depth):
        mod = mods[l]
        xf = _ffn(xf, norm_g[l, 0][None], mod, w_ffn_in_b, w_ffn_out_b, (l, 0), 0, S)
        if l < n_a:
            tn = _pick(D, 512)
            vec = pl.BlockSpec((1, tn), lambda i, j: (0, j))
            q, lf, k, v, sg = _proj(
                xf, norm_g[l, 1][None], mod, hgrn_w_in_b, (l,), [D, 0, 3 * D, 2 * D], D, tn,
                _pick(S, 512, 8), S, 3, 4,
                [lb_floor[l][None], one_m_lb[l][None]], [vec, vec],
                [tile(tn)] * 5,
                [jax.ShapeDtypeStruct((M, D), dt) for dt in (F32, F32, F32, BF16, BF16)],
                _hgrn_epilogue, "hgrn_proj")
            a = _hgrn(q, lf, k, v, sg, hgrn_out_gain[l][None], S)
            xf = _oproj(a, hgrn_w_out_b, (l,), xf, mod, 5, S)
        else:
            jl = l - n_a
            (qs,) = _group_proj(xf, norm_g[l, 1][None], mod, attn_w_q_b, (jl,), [0],
                                attn_q_gain[jl], B, S, n_heads, 3, 4, "q_proj")
            a = _attention(qs, ks, vs, B, S, n_heads)
            xf = _oproj(a, attn_w_o_b, (jl,), xf, mod, 5, S)
        xf = _ffn(xf, norm_g[l, 2][None], mod, w_ffn_in_b, w_ffn_out_b, (l, 1), 2, S)
        if l == n_a - 1:
            ks, vs = _group_proj(xf, kv_norm_g[None], kv_mod, w_kv_b, (), [0, GW], k_gain,
                                 B, S, n_heads, 0, 1, "kv_proj")
    return xf.reshape(B, S, D)
```

```python
import functools

import numpy as np
import jax
import jax.numpy as jnp
from jax import lax
from jax.experimental import pallas as pl
from jax.experimental.pallas import tpu as pltpu

F32 = jnp.float32
BF16 = jnp.bfloat16

EPS = 1e-6
MASK_VALUE = -1e30
LB_FLOOR = 1e-30
LOG2_E = 1.4426950408889634
HEAD_DIM = 128
ATTN_GROUPS = ((128, 1), (512, 4), (2048, 16))
ATTN_BLOCK = 128
HGRN_CHUNK = 128
HGRN_DIAG = 2
HGRN_HEADS_PER_STEP = 4

VMEM_LIMIT_BYTES = 56 * 1024 * 1024


def _pick(n, target, mult=128):
    if n <= target:
        return n
    d = (target // mult) * mult
    while d >= mult:
        if n % d == 0:
            return d
        d -= mult
    raise ValueError(f"no tile for {n}")


def _params(sem):
    return pltpu.CompilerParams(dimension_semantics=sem, vmem_limit_bytes=VMEM_LIMIT_BYTES)


def _silu(x):
    return x * (0.5 + 0.5 * jnp.tanh(0.5 * x))


def _dot(a, b):
    return jnp.dot(a, b, preferred_element_type=F32)


def _dot_nt(a, b):
    return lax.dot_general(a, b, (((1,), (1,)), ((), ())), preferred_element_type=F32)


NORM_ROWS = 32


def _norm_mod_into(hn_ref, x_ref, gain, shift, scale):
    gs = gain * (1 + scale)
    for r in range(0, x_ref.shape[0], NORM_ROWS):
        x = x_ref[r:r + NORM_ROWS, :]
        ms = jnp.mean(x * x, axis=-1, keepdims=True)
        hn_ref[r:r + NORM_ROWS, :] = (x * lax.rsqrt(ms + EPS) * gs + shift).astype(BF16)


def _head_norm(a, gain):
    outs = []
    for h in range(a.shape[1] // HEAD_DIM):
        ah = a[:, h * HEAD_DIM:(h + 1) * HEAD_DIM]
        ms = jnp.mean(ah * ah, axis=-1, keepdims=True)
        outs.append(ah * lax.rsqrt(ms + EPS) * gain)
    return outs[0] if len(outs) == 1 else jnp.concatenate(outs, axis=1)


def _ada_kernel(c_ref, w_ref, b_ref, o_ref):
    sc = _silu(c_ref[...]).astype(BF16)
    o_ref[0] = _dot(sc, w_ref[0].astype(BF16)) + b_ref[0]


def _ada(c, w, b):
    L, D, N = w.shape
    B = c.shape[0]
    tn = _pick(N, 1024)
    return pl.pallas_call(
        _ada_kernel,
        grid=(L, N // tn),
        in_specs=[
            pl.BlockSpec((B, D), lambda l, j: (0, 0)),
            pl.BlockSpec((1, D, tn), lambda l, j: (l, 0, j)),
            pl.BlockSpec((1, 1, tn), lambda l, j: (l, 0, j)),
        ],
        out_specs=pl.BlockSpec((1, B, tn), lambda l, j: (l, 0, j)),
        out_shape=jax.ShapeDtypeStruct((L, B, N), F32),
        compiler_params=_params(("parallel", "parallel")),
        name="ada",
    )(c, w, b.reshape(L, 1, N))


def _ffn_kernel(x_ref, g_ref, mod_ref, w_in_ref, w_out_ref, o_ref,
                hn_ref, wg_buf, wu_buf, wo_buf, sems, *, lead, sub, nj, tf, n_tiles):
    i = pl.program_id(0)
    F = nj * tf

    def copies(j, slot):
        cols = pl.ds(j * tf, tf)
        return (
            pltpu.make_async_copy(w_in_ref.at[lead + (slice(None), cols)], wg_buf.at[slot],
                                  sems.at[0, slot]),
            pltpu.make_async_copy(w_in_ref.at[lead + (slice(None), pl.ds(F + j * tf, tf))],
                                  wu_buf.at[slot], sems.at[1, slot]),
            pltpu.make_async_copy(w_out_ref.at[lead + (cols, slice(None))], wo_buf.at[slot],
                                  sems.at[2, slot]),
        )

    def start(j, slot):
        for c in copies(j, slot):
            c.start()

    @pl.when(i == 0)
    def _():
        start(0, 0)

    _norm_mod_into(hn_ref, x_ref, g_ref[...], mod_ref[0, 3 * sub:3 * sub + 1, :],
                   mod_ref[0, 3 * sub + 1:3 * sub + 2, :])
    hn = hn_ref[...]
    first = (i * nj) % 2 if nj % 2 else 0
    for j in range(nj):
        slot = (first + j) % 2
        if j + 1 < nj:
            start(j + 1, 1 - slot)
        else:
            @pl.when(i + 1 < n_tiles)
            def _():
                start(0, 1 - slot)
        for c in copies(j, slot):
            c.wait()
        hg = _dot(hn, wg_buf[slot])
        hu = _dot(hn, wu_buf[slot])
        act = (_silu(hg) * hu).astype(BF16)
        y = _dot(act, wo_buf[slot])
        if j == 0:
            o_ref[...] = y
        else:
            o_ref[...] += y
    gate = mod_ref[0, 3 * sub + 2:3 * sub + 3, :]
    o_ref[...] = x_ref[...] + (0.5 * (1 + gate)) * o_ref[...]


def _wspec(lead, block, index_map):
    return pl.BlockSpec((None,) * len(lead) + block, lambda i, j: lead + index_map(i, j))


def _ffn(x, gain, mod, w_in, w_out, lead, sub, seq):
    M, D = x.shape
    F = w_out.shape[-2]
    tm = _pick(seq, 512, 8)
    tf = _pick(F, 512)
    nj = F // tf
    per_b = seq // tm
    return pl.pallas_call(
        functools.partial(_ffn_kernel, lead=lead, sub=sub, nj=nj, tf=tf, n_tiles=M // tm),
        grid=(M // tm,),
        in_specs=[
            pl.BlockSpec((tm, D), lambda i: (i, 0)),
            pl.BlockSpec((1, D), lambda i: (0, 0)),
            pl.BlockSpec((1, 9, D), lambda i: (i // per_b, 0, 0)),
            pl.BlockSpec(memory_space=pl.ANY),
            pl.BlockSpec(memory_space=pl.ANY),
        ],
        out_specs=pl.BlockSpec((tm, D), lambda i: (i, 0)),
        out_shape=jax.ShapeDtypeStruct((M, D), F32),
        scratch_shapes=[pltpu.VMEM((tm, D), BF16),
                        pltpu.VMEM((2, D, tf), BF16), pltpu.VMEM((2, D, tf), BF16),
                        pltpu.VMEM((2, tf, D), BF16), pltpu.SemaphoreType.DMA((3, 2))],
        compiler_params=_params(("arbitrary",)),
        name="ffn",
    )(x, gain, mod, w_in, w_out)


HGRN_PROJ_COLS = 256


def _hgrn_proj_kernel(x_ref, g_ref, mod_ref, lbf_ref, oml_ref, w_ref,
                      q_ref, lf_ref, k_ref, v_ref, sg_ref, hn_ref, w_buf, sems,
                      *, lead, shift_row, scale_row, n_tiles):
    D = x_ref.shape[1]
    tn = HGRN_PROJ_COLS
    n_chunks = D // tn
    i = pl.program_id(0)
    sections = (D, 0, 3 * D, 2 * D)

    def copies(c, slot):
        return [pltpu.make_async_copy(w_ref.at[lead + (slice(None), pl.ds(off + c * tn, tn))],
                                      w_buf.at[slot, k], sems.at[k, slot])
                for k, off in enumerate(sections)]

    def start(c, slot):
        for cp in copies(c, slot):
            cp.start()

    @pl.when(i == 0)
    def _():
        start(0, 0)

    _norm_mod_into(hn_ref, x_ref, g_ref[...], mod_ref[0, shift_row:shift_row + 1, :],
                   mod_ref[0, scale_row:scale_row + 1, :])
    hn = hn_ref[...]
    first = (i * n_chunks) % 2 if n_chunks % 2 else 0
    for c in range(n_chunks):
        slot = (first + c) % 2
        if c + 1 < n_chunks:
            start(c + 1, 1 - slot)
        else:
            @pl.when(i + 1 < n_tiles)
            def _():
                start(0, 1 - slot)
        for cp in copies(c, slot):
            cp.wait()
        cols = slice(c * tn, (c + 1) * tn)
        ff, q, g, v = (_dot(hn, w_buf[slot, k]) for k in range(4))
        e = jnp.exp(-jnp.abs(ff))
        r = 1.0 / (1.0 + e)
        pos = ff >= 0
        one_m_lb = oml_ref[:, cols]
        lf_ref[:, cols] = jnp.log(lbf_ref[:, cols] + one_m_lb * jnp.where(pos, r, e * r)) * LOG2_E
        k_ref[:, cols] = one_m_lb * jnp.where(pos, e * r, r)
        q_ref[:, cols] = _silu(q)
        sg_ref[:, cols] = _silu(g).astype(BF16)
        v_ref[:, cols] = v.astype(BF16)


def _hgrn_proj(x, gain, mod, w, lead, lb_floor, one_m_lb, seq, shift_row, scale_row):
    M, D = x.shape
    tm = _pick(seq, 512, 8)
    per_b = seq // tm
    row = lambda: pl.BlockSpec((1, D), lambda i: (0, 0))
    tile = lambda: pl.BlockSpec((tm, D), lambda i: (i, 0))
    return pl.pallas_call(
        functools.partial(_hgrn_proj_kernel, lead=lead, shift_row=shift_row,
                          scale_row=scale_row, n_tiles=M // tm),
        grid=(M // tm,),
        in_specs=[tile(), row(), pl.BlockSpec((1, mod.shape[1], D), lambda i: (i // per_b, 0, 0)),
                  row(), row(), pl.BlockSpec(memory_space=pl.ANY)],
        out_specs=[tile() for _ in range(5)],
        out_shape=[jax.ShapeDtypeStruct((M, D), dt) for dt in (F32, F32, F32, BF16, BF16)],
        scratch_shapes=[pltpu.VMEM((tm, D), BF16),
                        pltpu.VMEM((2, 4, D, HGRN_PROJ_COLS), BF16),
                        pltpu.SemaphoreType.DMA((4, 2))],
        compiler_params=_params(("arbitrary",)),
        name="hgrn_proj",
    )(x, gain, mod, lb_floor, one_m_lb, w)


def _store_dilated(a, scr_ref, out_ref, dil):
    rows, width = a.shape
    for h in range(width // HEAD_DIM):
        ah = a[:, h * HEAD_DIM:(h + 1) * HEAD_DIM]
        if dil == 1:
            out_ref[0, h, 0] = ah.astype(BF16)
        else:
            scr_ref[h] = ah
            for r in range(dil):
                out_ref[0, h, r] = scr_ref[h, pl.ds(r, rows // dil, stride=dil), :].astype(BF16)


def _group_proj_kernel(x_ref, g_ref, mod_ref, gain_ref, w_ref, *refs, lead, col_offsets, width,
                       shift_row, scale_row, n_tiles):
    n_g, n_w = len(ATTN_GROUPS), len(col_offsets)
    out_refs = refs[:n_w * n_g]
    hn_ref, w_buf, scr_ref, sems = refs[n_w * n_g:]
    i = pl.program_id(0)
    order = sorted(range(n_g), key=lambda gi: -ATTN_GROUPS[gi][1])

    def copies(c, slot):
        return [pltpu.make_async_copy(
            w_ref.at[lead + (slice(None), pl.ds(off + order[c] * width, width))],
            w_buf.at[slot, k], sems.at[k, slot]) for k, off in enumerate(col_offsets)]

    def start(c, slot):
        for cp in copies(c, slot):
            cp.start()

    @pl.when(i == 0)
    def _():
        start(0, 0)

    _norm_mod_into(hn_ref, x_ref, g_ref[...], mod_ref[0, shift_row:shift_row + 1, :],
                   mod_ref[0, scale_row:scale_row + 1, :])
    hn = hn_ref[...]
    first = (i * n_g) % 2 if n_g % 2 else 0
    for c, gi in enumerate(order):
        slot = (first + c) % 2
        if c + 1 < n_g:
            start(c + 1, 1 - slot)
        else:
            @pl.when(i + 1 < n_tiles)
            def _():
                start(0, 1 - slot)
        for cp in copies(c, slot):
            cp.wait()
        dil = ATTN_GROUPS[gi][1]
        for k in range(n_w):
            a = _dot(hn, w_buf[slot, k])
            if k == 0:
                a = _head_norm(a, gain_ref[gi])
            _store_dilated(a, scr_ref.at[k], out_refs[k * n_g + gi], dil)


def _group_proj(x, gain, mod, w, lead, col_offsets, head_gain, batch, seq, n_heads,
                shift_row, scale_row, name):
    M, D = x.shape
    n_g, n_w = len(ATTN_GROUPS), len(col_offsets)
    width = n_heads * HEAD_DIM
    tm = _pick(seq, 512, 8 * max(d for _, d in ATTN_GROUPS))
    per_b = seq // tm
    n_mod = mod.shape[1]
    out_specs, out_shapes = [], []
    for _ in range(n_w):
        for _, dil in ATTN_GROUPS:
            out_specs.append(pl.BlockSpec((1, n_heads, dil, tm // dil, HEAD_DIM),
                                          lambda i: (i // per_b, 0, 0, i % per_b, 0)))
            out_shapes.append(
                jax.ShapeDtypeStruct((batch, n_heads, dil, seq // dil, HEAD_DIM), BF16))
    outs = pl.pallas_call(
        functools.partial(_group_proj_kernel, lead=lead, col_offsets=tuple(col_offsets),
                          width=width, shift_row=shift_row, scale_row=scale_row,
                          n_tiles=M // tm),
        grid=(M // tm,),
        in_specs=[
            pl.BlockSpec((tm, D), lambda i: (i, 0)),
            pl.BlockSpec((1, D), lambda i: (0, 0)),
            pl.BlockSpec((1, n_mod, D), lambda i: (i // per_b, 0, 0)),
            pl.BlockSpec((n_g, 1, HEAD_DIM), lambda i: (0, 0, 0)),
            pl.BlockSpec(memory_space=pl.ANY),
        ],
        out_specs=out_specs,
        out_shape=out_shapes,
        scratch_shapes=[pltpu.VMEM((tm, D), BF16), pltpu.VMEM((2, n_w, D, width), BF16),
                        pltpu.VMEM((n_w, n_heads, tm, HEAD_DIM), F32),
                        pltpu.SemaphoreType.DMA((n_w, 2))],
        compiler_params=_params(("arbitrary",)),
        name=name,
    )(x, gain, mod, head_gain[:, None, :], w)
    return [outs[k * n_g:(k + 1) * n_g] for k in range(n_w)]


def _hgrn_levels():
    P, out = 2 * HGRN_DIAG, []
    while P <= HGRN_CHUNK:
        out.append(P)
        P *= 2
    return out


def _hgrn_masks():
    t = np.arange(HGRN_CHUNK)
    rows = []
    for P in _hgrn_levels():
        same = (t[:, None] // P) == (t[None, :] // P)
        rows.append(same & ((t[:, None] % P) >= P // 2) & ((t[None, :] % P) < P // 2))
    return jnp.asarray(np.stack(rows), F32)


def _dot_tn(a, b):
    return lax.dot_general(a, b, (((0,), (0,)), ((), ())), preferred_element_type=F32)


def _hgrn_chunk(qs, lf, k, v, ST, tri, ones, delta, same_diag_block, tok, mask_ref):
    C, c = HGRN_CHUNK, HGRN_DIAG
    lf_hi = lf.astype(BF16)
    lf_lo = (lf - lf_hi.astype(F32)).astype(BF16)
    G = _dot(tri, lf_hi) + _dot(tri, lf_lo)

    W = [(qs * k).astype(BF16)]
    for d in range(1, c):
        ks = pltpu.roll(k, d, axis=0)
        Gs = pltpu.roll(G, d, axis=0)
        W.append((qs * ks * jnp.exp2(G - Gs)).astype(BF16))
    R = _dot(jnp.concatenate(W, axis=0), ones)
    A = jnp.zeros((C, C), F32)
    for d in range(c):
        A = jnp.where(delta == d, R[d * C:(d + 1) * C], A)
    A = jnp.where(same_diag_block, A, 0.0)

    for li, P in enumerate(_hgrn_levels()):
        half = P // 2
        if half % 8 == 0:
            src, expo = [], []
            for b in range(C // P):
                lo, mid, hi = b * P, b * P + half, (b + 1) * P
                g_mid = G[mid - 1:mid, :]
                src += [k[lo:mid], qs[mid:hi]]
                expo += [g_mid - G[lo:mid], G[mid:hi] - g_mid]
            Z = jnp.concatenate(src, axis=0) * jnp.exp2(jnp.concatenate(expo, axis=0))
        else:
            Ge = jnp.concatenate(
                [jnp.broadcast_to(G[b * P + half - 1:b * P + half, :], (P, HEAD_DIM))
                 for b in range(C // P)], axis=0)
            second = (tok % P) >= half
            Z = jnp.where(second, qs, k) * jnp.exp2(jnp.where(second, 1.0, -1.0) * (G - Ge))
        Z = Z.astype(BF16)
        A = A + mask_ref[li] * _dot_nt(Z, Z)

    o = _dot(A.astype(BF16), v) + _dot_nt((qs * jnp.exp2(G)).astype(BF16), ST.astype(BF16))
    g_last = G[C - 1:C, :]
    k_dec = (k * jnp.exp2(g_last - G)).astype(BF16)
    ST_new = ST * jnp.exp2(g_last) + _dot_tn(v, k_dec)
    return o, ST_new


def _hgrn_kernel(q_ref, lf_ref, k_ref, v_ref, sg_ref, gain_ref, mask_ref, o_ref, st_ref,
                 *, n_chunks, n_heads):
    C, c = HGRN_CHUNK, HGRN_DIAG
    row = lax.broadcasted_iota(jnp.int32, (C, C), 0)
    col = lax.broadcasted_iota(jnp.int32, (C, C), 1)
    tri = (col <= row).astype(BF16)
    ones = jnp.ones((HEAD_DIM, C), BF16)
    delta = row - col
    same_diag_block = (row // c) == (col // c)
    tok = lax.broadcasted_iota(jnp.int32, (C, 1), 0)
    gain = gain_ref[...]
    st_ref[...] = jnp.zeros_like(st_ref)

    def chunk(ci, carry):
        rows = pl.ds(pl.multiple_of(ci * C, C), C)
        for h in range(n_heads):
            sl = slice(h * HEAD_DIM, (h + 1) * HEAD_DIM)
            o, S_new = _hgrn_chunk(q_ref[rows, sl], lf_ref[rows, sl], k_ref[rows, sl],
                                   v_ref[rows, sl], st_ref[h], tri, ones, delta,
                                   same_diag_block, tok, mask_ref)
            st_ref[h] = S_new
            ms = jnp.mean(o * o, axis=-1, keepdims=True)
            on = o * lax.rsqrt(ms + EPS) * gain
            o_ref[rows, sl] = (on * sg_ref[rows, sl].astype(F32)).astype(BF16)
        return carry

    lax.fori_loop(0, n_chunks, chunk, 0, unroll=4)


def _hgrn(q, lf, k, v, sg, out_gain, seq):
    M, D = q.shape
    hb = min(HGRN_HEADS_PER_STEP, D // HEAD_DIM)
    masks = _hgrn_masks()
    blk = lambda: pl.BlockSpec((seq, hb * HEAD_DIM), lambda b, h: (b, h))
    return pl.pallas_call(
        functools.partial(_hgrn_kernel, n_chunks=seq // HGRN_CHUNK, n_heads=hb),
        grid=(M // seq, D // (hb * HEAD_DIM)),
        in_specs=[blk(), blk(), blk(), blk(), blk(),
                  pl.BlockSpec((1, HEAD_DIM), lambda b, h: (0, 0)),
                  pl.BlockSpec(masks.shape, lambda b, h: (0, 0, 0))],
        out_specs=blk(),
        out_shape=jax.ShapeDtypeStruct((M, D), BF16),
        scratch_shapes=[pltpu.VMEM((hb, HEAD_DIM, HEAD_DIM), F32)],
        compiler_params=_params(("parallel", "parallel")),
        name="hgrn",
    )(q, lf, k, v, sg, out_gain, masks)


def _oproj_kernel(a_ref, w_ref, x_ref, mod_ref, o_ref, *, gate_row):
    gate = mod_ref[0, gate_row:gate_row + 1, :]
    o_ref[...] = x_ref[...] + (1 + gate) * _dot(a_ref[...], w_ref[...])


def _oproj(a, w, lead, x, mod, gate_row, seq):
    M, K = a.shape
    D = w.shape[-1]
    tm = _pick(seq, 1024, 8)
    tn = _pick(D, 1024)
    per_b = seq // tm
    return pl.pallas_call(
        functools.partial(_oproj_kernel, gate_row=gate_row),
        grid=(M // tm, D // tn),
        in_specs=[
            pl.BlockSpec((tm, K), lambda i, j: (i, 0)),
            _wspec(lead, (K, tn), lambda i, j: (0, j)),
            pl.BlockSpec((tm, tn), lambda i, j: (i, j)),
            pl.BlockSpec((1, 9, tn), lambda i, j: (i // per_b, 0, j)),
        ],
        out_specs=pl.BlockSpec((tm, tn), lambda i, j: (i, j)),
        out_shape=jax.ShapeDtypeStruct((M, D), F32),
        compiler_params=_params(("parallel", "arbitrary")),
        name="oproj",
    )(a, w, x, mod)


def _attn_kernel(*refs, seq):
    n_g = len(ATTN_GROUPS)
    qkv = refs[:3 * n_g]
    out_ref = refs[3 * n_g]
    s_scr, p_scr, o_scr, m_scr, l_scr = refs[3 * n_g + 1:]
    T = ATTN_BLOCK
    scale = HEAD_DIM ** -0.5
    row = lax.broadcasted_iota(jnp.int32, (T, T), 0)
    col = lax.broadcasted_iota(jnp.int32, (T, T), 1)
    mask_c = col <= row
    mask_p = col >= row

    for gi, (_, dil) in enumerate(ATTN_GROUPS):
        q_ref, k_ref, v_ref = qkv[3 * gi:3 * gi + 3]
        blocks = [(r, n) for r in range(dil) for n in range(seq // dil // T)]

        def keys(n):
            return slice(max(n - 1, 0) * T, (n + 1) * T)

        def tokens(r, n):
            return pl.ds(n * T * dil + r, T, stride=dil) if dil > 1 else pl.ds(n * T, T)

        for bi, (r, n) in enumerate(blocks):
            s = _dot_nt(q_ref[0, 0, r, n * T:(n + 1) * T, :], k_ref[0, 0, r, keys(n), :]) * scale
            s_c = jnp.where(mask_c, s[:, -T:], MASK_VALUE)
            m = jnp.max(s_c, axis=-1, keepdims=True)
            s_scr[bi, :, T:] = s_c
            if n > 0:
                s_p = jnp.where(mask_p, s[:, :T], MASK_VALUE)
                m = jnp.maximum(m, jnp.max(s_p, axis=-1, keepdims=True))
                s_scr[bi, :, :T] = s_p
            m_scr[gi, tokens(r, n), :] = jnp.broadcast_to(m, (T, HEAD_DIM))
        for bi, (r, n) in enumerate(blocks):
            m = m_scr[gi, tokens(r, n), :]
            p_scr[bi, :, T:] = jnp.where(
                mask_c, jnp.exp(s_scr[bi, :, T:] - m), 0.0).astype(BF16)
            if n > 0:
                p_scr[bi, :, :T] = jnp.where(
                    mask_p, jnp.exp(s_scr[bi, :, :T] - m), 0.0).astype(BF16)
        for bi, (r, n) in enumerate(blocks):
            p = p_scr[bi] if n > 0 else p_scr[bi, :, T:]
            v = v_ref[0, 0, r, keys(n), :]
            o = _dot(p, jnp.concatenate([v, jnp.ones_like(v)], axis=1))
            o_scr[gi, tokens(r, n), :] = o[:, :HEAD_DIM]
            l_scr[gi, tokens(r, n), :] = o[:, HEAD_DIM:]

    for t in range(seq // T):
        rows = pl.ds(t * T, T)
        ms = [m_scr[gi, rows, :] for gi in range(n_g)]
        mx = functools.reduce(jnp.maximum, ms)
        ws = [jnp.exp(m - mx) for m in ms]
        num = sum(w * o_scr[gi, rows, :] for gi, w in enumerate(ws))
        den = sum(w * l_scr[gi, rows, :] for gi, w in enumerate(ws))
        out_ref[rows, :] = (num / den).astype(BF16)


def _attention(qs, ks, vs, batch, seq, n_heads):
    n_g = len(ATTN_GROUPS)
    in_specs, args = [], []
    for gi, (_, dil) in enumerate(ATTN_GROUPS):
        spec = pl.BlockSpec((1, 1, dil, seq // dil, HEAD_DIM), lambda b, h: (b, h, 0, 0, 0))
        in_specs += [spec, spec, spec]
        args += [qs[gi], ks[gi], vs[gi]]
    return pl.pallas_call(
        functools.partial(_attn_kernel, seq=seq),
        grid=(batch, n_heads),
        in_specs=in_specs,
        out_specs=pl.BlockSpec((seq, HEAD_DIM), lambda b, h: (b, h)),
        out_shape=jax.ShapeDtypeStruct((batch * seq, n_heads * HEAD_DIM), BF16),
        scratch_shapes=[pltpu.VMEM((seq // ATTN_BLOCK, ATTN_BLOCK, 2 * ATTN_BLOCK), F32),
                        pltpu.VMEM((seq // ATTN_BLOCK, ATTN_BLOCK, 2 * ATTN_BLOCK), BF16)]
        + [pltpu.VMEM((n_g, seq, HEAD_DIM), F32)] * 3,
        compiler_params=_params(("parallel", "parallel")),
        name="attn",
    )(*args)


def kernel(x, c, norm_g, w_ada, b_ada, w_ffn_in, w_ffn_out, hgrn_w_in, hgrn_w_out, hgrn_lb_logits, hgrn_out_gain, kv_norm_g, kv_w_ada, kv_b_ada, w_kv, k_gain, attn_w_q, attn_q_gain, attn_w_o):
    B, S, D = x.shape
    depth = w_ada.shape[0]
    n_a = hgrn_w_in.shape[0]
    n_groups = k_gain.shape[0]
    width = attn_w_o.shape[1]
    n_heads = width // HEAD_DIM
    GW = n_groups * width
    M = B * S
    assert n_groups == len(ATTN_GROUPS)
    assert all(win // dil == ATTN_BLOCK and S % (dil * ATTN_BLOCK) == 0 for win, dil in ATTN_GROUPS)

    mods = _ada(c, w_ada, b_ada).reshape(depth, B, 9, D)
    kv_mod = _ada(c, kv_w_ada[None], kv_b_ada[None]).reshape(B, 2, D)

    p = jax.nn.softmax(hgrn_lb_logits.astype(F32), axis=0)
    lb = jnp.cumsum(p, axis=0) - p[0]
    lb_floor = jnp.maximum(lb, LB_FLOOR)
    one_m_lb = 1 - lb
    w_ffn_in_b, w_ffn_out_b = w_ffn_in.astype(BF16), w_ffn_out.astype(BF16)
    hgrn_w_in_b, hgrn_w_out_b = hgrn_w_in.astype(BF16), hgrn_w_out.astype(BF16)
    attn_w_q_b, attn_w_o_b = attn_w_q.astype(BF16), attn_w_o.astype(BF16)
    w_kv_b = w_kv.astype(BF16)


    xf = x.reshape(M, D)
    ks = vs = None
    for l in range(depth):
        mod = mods[l]
        xf = _ffn(xf, norm_g[l, 0][None], mod, w_ffn_in_b, w_ffn_out_b, (l, 0), 0, S)
        if l < n_a:
            q, lf, k, v, sg = _hgrn_proj(xf, norm_g[l, 1][None], mod, hgrn_w_in_b, (l,),
                                         lb_floor[l][None], one_m_lb[l][None], S, 3, 4)
            a = _hgrn(q, lf, k, v, sg, hgrn_out_gain[l][None], S)
            xf = _oproj(a, hgrn_w_out_b, (l,), xf, mod, 5, S)
        else:
            jl = l - n_a
            (qs,) = _group_proj(xf, norm_g[l, 1][None], mod, attn_w_q_b, (jl,), [0],
                                attn_q_gain[jl], B, S, n_heads, 3, 4, "q_proj")
            a = _attention(qs, ks, vs, B, S, n_heads)
            xf = _oproj(a, attn_w_o_b, (jl,), xf, mod, 5, S)
        xf = _ffn(xf, norm_g[l, 2][None], mod, w_ffn_in_b, w_ffn_out_b, (l, 1), 2, S)
        if l == n_a - 1:
            ks, vs = _group_proj(xf, kv_norm_g[None], kv_mod, w_kv_b, (), [0, GW], k_gain,
                                 B, S, n_heads, 0, 1, "kv_proj")
    return xf.reshape(B, S, D)
```

```python
import functools

import numpy as np
import jax
import jax.numpy as jnp
from jax import lax
from jax.experimental import pallas as pl
from jax.experimental.pallas import tpu as pltpu

F32 = jnp.float32
BF16 = jnp.bfloat16

EPS = 1e-6
MASK_VALUE = -1e30
LB_FLOOR = 1e-30
LOG2_E = 1.4426950408889634
HEAD_DIM = 128
ATTN_GROUPS = ((128, 1), (512, 4), (2048, 16))
ATTN_BLOCK = 128
HGRN_CHUNK = 128
HGRN_DIAG = 2
HGRN_HEADS_PER_STEP = 4
HGRN_UNROLL = 4

LANES = 128
SUBLANES = 8
V7X_VMEM_BYTES = 64 * 1024 * 1024
VMEM_LIMIT_BYTES = V7X_VMEM_BYTES - 8 * 1024 * 1024
TOKEN_TILE = 512
COLUMN_TILE = 512
FFN_CHUNK = 512
WIDE_TILE = 1024


def _pick(n, target, mult=LANES):
    if n <= target:
        return n
    d = (target // mult) * mult
    while d >= mult:
        if n % d == 0:
            return d
        d -= mult
    raise ValueError(f"no tile for {n}")


def _params(sem):
    return pltpu.CompilerParams(dimension_semantics=sem, vmem_limit_bytes=VMEM_LIMIT_BYTES)


def _silu(x):
    return x * (0.5 + 0.5 * jnp.tanh(0.5 * x))


def _dot(a, b):
    return jnp.dot(a, b, preferred_element_type=F32)


def _dot_nt(a, b):
    return lax.dot_general(a, b, (((1,), (1,)), ((), ())), preferred_element_type=F32)


NORM_ROWS = 32


def _norm_mod_into(hn_ref, x_ref, gain, shift, scale):
    gs = gain * (1 + scale)
    for r in range(0, x_ref.shape[0], NORM_ROWS):
        x = x_ref[r:r + NORM_ROWS, :]
        ms = jnp.mean(x * x, axis=-1, keepdims=True)
        hn_ref[r:r + NORM_ROWS, :] = (x * lax.rsqrt(ms + EPS) * gs + shift).astype(BF16)


def _head_norm(a, gain):
    outs = []
    for h in range(a.shape[1] // HEAD_DIM):
        ah = a[:, h * HEAD_DIM:(h + 1) * HEAD_DIM]
        ms = jnp.mean(ah * ah, axis=-1, keepdims=True)
        outs.append(ah * lax.rsqrt(ms + EPS) * gain)
    return outs[0] if len(outs) == 1 else jnp.concatenate(outs, axis=1)


def _ada_kernel(c_ref, w_ref, b_ref, o_ref):
    sc = _silu(c_ref[...]).astype(BF16)
    o_ref[0] = _dot(sc, w_ref[0].astype(BF16)) + b_ref[0]


def _ada(c, w, b):
    L, D, N = w.shape
    B = c.shape[0]
    tn = _pick(N, WIDE_TILE)
    return pl.pallas_call(
        _ada_kernel,
        grid=(L, N // tn),
        in_specs=[
            pl.BlockSpec((B, D), lambda l, j: (0, 0)),
            pl.BlockSpec((1, D, tn), lambda l, j: (l, 0, j)),
            pl.BlockSpec((1, 1, tn), lambda l, j: (l, 0, j)),
        ],
        out_specs=pl.BlockSpec((1, B, tn), lambda l, j: (l, 0, j)),
        out_shape=jax.ShapeDtypeStruct((L, B, N), F32),
        compiler_params=_params(("parallel", "parallel")),
        name="ada",
    )(c, w, b.reshape(L, 1, N))


def _ffn_kernel(x_ref, g_ref, mod_ref, w_in_ref, w_out_ref, o_ref,
                hn_ref, wg_buf, wu_buf, wo_buf, sems, *, lead, sub, nj, tf, n_tiles):
    i = pl.program_id(0)
    F = nj * tf

    def copies(j, slot):
        cols = pl.ds(j * tf, tf)
        return (
            pltpu.make_async_copy(w_in_ref.at[lead + (slice(None), cols)], wg_buf.at[slot],
                                  sems.at[0, slot]),
            pltpu.make_async_copy(w_in_ref.at[lead + (slice(None), pl.ds(F + j * tf, tf))],
                                  wu_buf.at[slot], sems.at[1, slot]),
            pltpu.make_async_copy(w_out_ref.at[lead + (cols, slice(None))], wo_buf.at[slot],
                                  sems.at[2, slot]),
        )

    def start(j, slot):
        for c in copies(j, slot):
            c.start()

    @pl.when(i == 0)
    def _():
        start(0, 0)

    _norm_mod_into(hn_ref, x_ref, g_ref[...], mod_ref[0, 3 * sub:3 * sub + 1, :],
                   mod_ref[0, 3 * sub + 1:3 * sub + 2, :])
    hn = hn_ref[...]
    first = (i * nj) % 2 if nj % 2 else 0
    for j in range(nj):
        slot = (first + j) % 2
        if j + 1 < nj:
            start(j + 1, 1 - slot)
        else:
            @pl.when(i + 1 < n_tiles)
            def _():
                start(0, 1 - slot)
        for c in copies(j, slot):
            c.wait()
        hg = _dot(hn, wg_buf[slot])
        hu = _dot(hn, wu_buf[slot])
        act = (_silu(hg) * hu).astype(BF16)
        y = _dot(act, wo_buf[slot])
        if j == 0:
            o_ref[...] = y
        else:
            o_ref[...] += y
    gate = mod_ref[0, 3 * sub + 2:3 * sub + 3, :]
    o_ref[...] = x_ref[...] + (0.5 * (1 + gate)) * o_ref[...]


def _wspec(lead, block, index_map):
    return pl.BlockSpec((None,) * len(lead) + block, lambda i, j: lead + index_map(i, j))


def _ffn(x, gain, mod, w_in, w_out, lead, sub, seq):
    M, D = x.shape
    F = w_out.shape[-2]
    tm = _pick(seq, TOKEN_TILE, SUBLANES)
    tf = _pick(F, FFN_CHUNK)
    nj = F // tf
    per_b = seq // tm
    return pl.pallas_call(
        functools.partial(_ffn_kernel, lead=lead, sub=sub, nj=nj, tf=tf, n_tiles=M // tm),
        grid=(M // tm,),
        in_specs=[
            pl.BlockSpec((tm, D), lambda i: (i, 0)),
            pl.BlockSpec((1, D), lambda i: (0, 0)),
            pl.BlockSpec((1, 9, D), lambda i: (i // per_b, 0, 0)),
            pl.BlockSpec(memory_space=pl.ANY),
            pl.BlockSpec(memory_space=pl.ANY),
        ],
        out_specs=pl.BlockSpec((tm, D), lambda i: (i, 0)),
        out_shape=jax.ShapeDtypeStruct((M, D), F32),
        scratch_shapes=[pltpu.VMEM((tm, D), BF16),
                        pltpu.VMEM((2, D, tf), BF16), pltpu.VMEM((2, D, tf), BF16),
                        pltpu.VMEM((2, tf, D), BF16), pltpu.SemaphoreType.DMA((3, 2))],
        compiler_params=_params(("arbitrary",)),
        name="ffn",
    )(x, gain, mod, w_in, w_out)


def _hgrn_proj_kernel(x_ref, g_ref, mod_ref, wf_ref, wq_ref, wg_ref, wv_ref, lbf_ref, oml_ref,
                      q_ref, lf_ref, k_ref, v_ref, sg_ref, hn_ref, *, shift_row, scale_row):
    @pl.when(pl.program_id(1) == 0)
    def _():
        _norm_mod_into(hn_ref, x_ref, g_ref[...], mod_ref[0, shift_row:shift_row + 1, :],
                       mod_ref[0, scale_row:scale_row + 1, :])

    hn = hn_ref[...]
    ff, q, g, v = (_dot(hn, w[...]) for w in (wf_ref, wq_ref, wg_ref, wv_ref))
    e = jnp.exp(-jnp.abs(ff))
    r = 1.0 / (1.0 + e)
    pos = ff >= 0
    one_m_lb = oml_ref[...]
    lf_ref[...] = jnp.log(lbf_ref[...] + one_m_lb * jnp.where(pos, r, e * r)) * LOG2_E
    k_ref[...] = one_m_lb * jnp.where(pos, e * r, r)
    q_ref[...] = _silu(q)
    sg_ref[...] = _silu(g).astype(BF16)
    v_ref[...] = v.astype(BF16)


def _hgrn_proj(x, gain, mod, w, lead, lb_floor, one_m_lb, seq, shift_row, scale_row):
    M, D = x.shape
    tm = _pick(seq, TOKEN_TILE, SUBLANES)
    tn = _pick(D, COLUMN_TILE)
    per_b = seq // tm
    section = lambda k: _wspec(lead, (D, tn), lambda i, j: (0, k * (D // tn) + j))
    vec = lambda: pl.BlockSpec((1, tn), lambda i, j: (0, j))
    tile = lambda: pl.BlockSpec((tm, tn), lambda i, j: (i, j))
    return pl.pallas_call(
        functools.partial(_hgrn_proj_kernel, shift_row=shift_row, scale_row=scale_row),
        grid=(M // tm, D // tn),
        in_specs=[
            pl.BlockSpec((tm, D), lambda i, j: (i, 0)),
            pl.BlockSpec((1, D), lambda i, j: (0, 0)),
            pl.BlockSpec((1, mod.shape[1], D), lambda i, j: (i // per_b, 0, 0)),
            section(1), section(0), section(3), section(2), vec(), vec(),
        ],
        out_specs=[tile() for _ in range(5)],
        out_shape=[jax.ShapeDtypeStruct((M, D), dt) for dt in (F32, F32, F32, BF16, BF16)],
        scratch_shapes=[pltpu.VMEM((tm, D), BF16)],
        compiler_params=_params(("parallel", "arbitrary")),
        name="hgrn_proj",
    )(x, gain, mod, w, w, w, w, lb_floor, one_m_lb)


def _store_dilated(a, scr_ref, out_ref, dil):
    rows, width = a.shape
    for h in range(width // HEAD_DIM):
        ah = a[:, h * HEAD_DIM:(h + 1) * HEAD_DIM]
        if dil == 1:
            out_ref[0, h, 0] = ah.astype(BF16)
        else:
            scr_ref[h] = ah
            for r in range(dil):
                out_ref[0, h, r] = scr_ref[h, pl.ds(r, rows // dil, stride=dil), :].astype(BF16)


def _group_proj_kernel(x_ref, g_ref, mod_ref, gain_ref, w_ref, *refs, lead, col_offsets, width,
                       shift_row, scale_row, n_tiles):
    n_g, n_w = len(ATTN_GROUPS), len(col_offsets)
    out_refs = refs[:n_w * n_g]
    hn_ref, w_buf, scr_ref, sems = refs[n_w * n_g:]
    i = pl.program_id(0)
    order = sorted(range(n_g), key=lambda gi: -ATTN_GROUPS[gi][1])

    def copies(c, slot):
        return [pltpu.make_async_copy(
            w_ref.at[lead + (slice(None), pl.ds(off + order[c] * width, width))],
            w_buf.at[slot, k], sems.at[k, slot]) for k, off in enumerate(col_offsets)]

    def start(c, slot):
        for cp in copies(c, slot):
            cp.start()

    @pl.when(i == 0)
    def _():
        start(0, 0)

    _norm_mod_into(hn_ref, x_ref, g_ref[...], mod_ref[0, shift_row:shift_row + 1, :],
                   mod_ref[0, scale_row:scale_row + 1, :])
    hn = hn_ref[...]
    first = (i * n_g) % 2 if n_g % 2 else 0
    for c, gi in enumerate(order):
        slot = (first + c) % 2
        if c + 1 < n_g:
            start(c + 1, 1 - slot)
        else:
            @pl.when(i + 1 < n_tiles)
            def _():
                start(0, 1 - slot)
        for cp in copies(c, slot):
            cp.wait()
        dil = ATTN_GROUPS[gi][1]
        for k in range(n_w):
            a = _dot(hn, w_buf[slot, k])
            if k == 0:
                a = _head_norm(a, gain_ref[gi])
            _store_dilated(a, scr_ref.at[k], out_refs[k * n_g + gi], dil)


def _group_proj(x, gain, mod, w, lead, col_offsets, head_gain, batch, seq, n_heads,
                shift_row, scale_row, name):
    M, D = x.shape
    n_g, n_w = len(ATTN_GROUPS), len(col_offsets)
    width = n_heads * HEAD_DIM
    tm = _pick(seq, TOKEN_TILE, SUBLANES * max(d for _, d in ATTN_GROUPS))
    per_b = seq // tm
    n_mod = mod.shape[1]
    out_specs, out_shapes = [], []
    for _ in range(n_w):
        for _, dil in ATTN_GROUPS:
            out_specs.append(pl.BlockSpec((1, n_heads, dil, tm // dil, HEAD_DIM),
                                          lambda i: (i // per_b, 0, 0, i % per_b, 0)))
            out_shapes.append(
                jax.ShapeDtypeStruct((batch, n_heads, dil, seq // dil, HEAD_DIM), BF16))
    outs = pl.pallas_call(
        functools.partial(_group_proj_kernel, lead=lead, col_offsets=tuple(col_offsets),
                          width=width, shift_row=shift_row, scale_row=scale_row,
                          n_tiles=M // tm),
        grid=(M // tm,),
        in_specs=[
            pl.BlockSpec((tm, D), lambda i: (i, 0)),
            pl.BlockSpec((1, D), lambda i: (0, 0)),
            pl.BlockSpec((1, n_mod, D), lambda i: (i // per_b, 0, 0)),
            pl.BlockSpec((n_g, 1, HEAD_DIM), lambda i: (0, 0, 0)),
            pl.BlockSpec(memory_space=pl.ANY),
        ],
        out_specs=out_specs,
        out_shape=out_shapes,
        scratch_shapes=[pltpu.VMEM((tm, D), BF16), pltpu.VMEM((2, n_w, D, width), BF16),
                        pltpu.VMEM((n_w, n_heads, tm, HEAD_DIM), F32),
                        pltpu.SemaphoreType.DMA((n_w, 2))],
        compiler_params=_params(("arbitrary",)),
        name=name,
    )(x, gain, mod, head_gain[:, None, :], w)
    return [outs[k * n_g:(k + 1) * n_g] for k in range(n_w)]


def _hgrn_levels():
    P, out = 2 * HGRN_DIAG, []
    while P <= HGRN_CHUNK:
        out.append(P)
        P *= 2
    return out


def _hgrn_masks():
    t = np.arange(HGRN_CHUNK)
    rows = []
    for P in _hgrn_levels():
        same = (t[:, None] // P) == (t[None, :] // P)
        rows.append(same & ((t[:, None] % P) >= P // 2) & ((t[None, :] % P) < P // 2))
    return jnp.asarray(np.stack(rows), F32)


def _dot_tn(a, b):
    return lax.dot_general(a, b, (((0,), (0,)), ((), ())), preferred_element_type=F32)


def _hgrn_chunk(qs, lf, k, v, ST, tri, ones, delta, same_diag_block, tok, mask_ref):
    C, c = HGRN_CHUNK, HGRN_DIAG
    lf_hi = lf.astype(BF16)
    lf_lo = (lf - lf_hi.astype(F32)).astype(BF16)
    G = _dot(tri, lf_hi) + _dot(tri, lf_lo)

    W = [(qs * k).astype(BF16)]
    for d in range(1, c):
        ks = pltpu.roll(k, d, axis=0)
        Gs = pltpu.roll(G, d, axis=0)
        W.append((qs * ks * jnp.exp2(G - Gs)).astype(BF16))
    R = _dot(jnp.concatenate(W, axis=0), ones)
    A = jnp.zeros((C, C), F32)
    for d in range(c):
        A = jnp.where(delta == d, R[d * C:(d + 1) * C], A)
    A = jnp.where(same_diag_block, A, 0.0)

    for li, P in enumerate(_hgrn_levels()):
        half = P // 2
        if half % 8 == 0:
            src, expo = [], []
            for b in range(C // P):
                lo, mid, hi = b * P, b * P + half, (b + 1) * P
                g_mid = G[mid - 1:mid, :]
                src += [k[lo:mid], qs[mid:hi]]
                expo += [g_mid - G[lo:mid], G[mid:hi] - g_mid]
            Z = jnp.concatenate(src, axis=0) * jnp.exp2(jnp.concatenate(expo, axis=0))
        else:
            Ge = jnp.concatenate(
                [jnp.broadcast_to(G[b * P + half - 1:b * P + half, :], (P, HEAD_DIM))
                 for b in range(C // P)], axis=0)
            second = (tok % P) >= half
            Z = jnp.where(second, qs, k) * jnp.exp2(jnp.where(second, 1.0, -1.0) * (G - Ge))
        Z = Z.astype(BF16)
        A = A + mask_ref[li] * _dot_nt(Z, Z)

    o = _dot(A.astype(BF16), v) + _dot_nt((qs * jnp.exp2(G)).astype(BF16), ST.astype(BF16))
    g_last = G[C - 1:C, :]
    k_dec = (k * jnp.exp2(g_last - G)).astype(BF16)
    ST_new = ST * jnp.exp2(g_last) + _dot_tn(v, k_dec)
    return o, ST_new


def _hgrn_kernel(q_ref, lf_ref, k_ref, v_ref, sg_ref, gain_ref, mask_ref, o_ref, st_ref,
                 *, n_chunks, n_heads):
    C, c = HGRN_CHUNK, HGRN_DIAG
    row = lax.broadcasted_iota(jnp.int32, (C, C), 0)
    col = lax.broadcasted_iota(jnp.int32, (C, C), 1)
    tri = (col <= row).astype(BF16)
    ones = jnp.ones((HEAD_DIM, C), BF16)
    delta = row - col
    same_diag_block = (row // c) == (col // c)
    tok = lax.broadcasted_iota(jnp.int32, (C, 1), 0)
    gain = gain_ref[...]
    st_ref[...] = jnp.zeros_like(st_ref)

    def chunk(ci, carry):
        rows = pl.ds(pl.multiple_of(ci * C, C), C)
        for h in range(n_heads):
            sl = slice(h * HEAD_DIM, (h + 1) * HEAD_DIM)
            o, S_new = _hgrn_chunk(q_ref[rows, sl], lf_ref[rows, sl], k_ref[rows, sl],
                                   v_ref[rows, sl], st_ref[h], tri, ones, delta,
                                   same_diag_block, tok, mask_ref)
            st_ref[h] = S_new
            ms = jnp.mean(o * o, axis=-1, keepdims=True)
            on = o * lax.rsqrt(ms + EPS) * gain
            o_ref[rows, sl] = (on * sg_ref[rows, sl].astype(F32)).astype(BF16)
        return carry

    lax.fori_loop(0, n_chunks, chunk, 0, unroll=HGRN_UNROLL)


def _hgrn(q, lf, k, v, sg, out_gain, seq):
    M, D = q.shape
    hb = min(HGRN_HEADS_PER_STEP, D // HEAD_DIM)
    masks = _hgrn_masks()
    blk = lambda: pl.BlockSpec((seq, hb * HEAD_DIM), lambda b, h: (b, h))
    return pl.pallas_call(
        functools.partial(_hgrn_kernel, n_chunks=seq // HGRN_CHUNK, n_heads=hb),
        grid=(M // seq, D // (hb * HEAD_DIM)),
        in_specs=[blk(), blk(), blk(), blk(), blk(),
                  pl.BlockSpec((1, HEAD_DIM), lambda b, h: (0, 0)),
                  pl.BlockSpec(masks.shape, lambda b, h: (0, 0, 0))],
        out_specs=blk(),
        out_shape=jax.ShapeDtypeStruct((M, D), BF16),
        scratch_shapes=[pltpu.VMEM((hb, HEAD_DIM, HEAD_DIM), F32)],
        compiler_params=_params(("parallel", "parallel")),
        name="hgrn",
    )(q, lf, k, v, sg, out_gain, masks)


def _oproj_kernel(a_ref, w_ref, x_ref, mod_ref, o_ref, *, gate_row):
    gate = mod_ref[0, gate_row:gate_row + 1, :]
    o_ref[...] = x_ref[...] + (1 + gate) * _dot(a_ref[...], w_ref[...])


def _oproj(a, w, lead, x, mod, gate_row, seq):
    M, K = a.shape
    D = w.shape[-1]
    tm = _pick(seq, WIDE_TILE, SUBLANES)
    tn = _pick(D, WIDE_TILE)
    per_b = seq // tm
    return pl.pallas_call(
        functools.partial(_oproj_kernel, gate_row=gate_row),
        grid=(M // tm, D // tn),
        in_specs=[
            pl.BlockSpec((tm, K), lambda i, j: (i, 0)),
            _wspec(lead, (K, tn), lambda i, j: (0, j)),
            pl.BlockSpec((tm, tn), lambda i, j: (i, j)),
            pl.BlockSpec((1, 9, tn), lambda i, j: (i // per_b, 0, j)),
        ],
        out_specs=pl.BlockSpec((tm, tn), lambda i, j: (i, j)),
        out_shape=jax.ShapeDtypeStruct((M, D), F32),
        compiler_params=_params(("parallel", "arbitrary")),
        name="oproj",
    )(a, w, x, mod)


def _attn_kernel(*refs, seq):
    n_g = len(ATTN_GROUPS)
    qkv = refs[:3 * n_g]
    out_ref = refs[3 * n_g]
    s_scr, p_scr, o_scr, m_scr, l_scr = refs[3 * n_g + 1:]
    T = ATTN_BLOCK
    scale = HEAD_DIM ** -0.5
    row = lax.broadcasted_iota(jnp.int32, (T, T), 0)
    col = lax.broadcasted_iota(jnp.int32, (T, T), 1)
    mask_c = col <= row
    mask_p = col >= row

    for gi, (_, dil) in enumerate(ATTN_GROUPS):
        q_ref, k_ref, v_ref = qkv[3 * gi:3 * gi + 3]
        blocks = [(r, n) for r in range(dil) for n in range(seq // dil // T)]

        def keys(n):
            return slice(max(n - 1, 0) * T, (n + 1) * T)

        def tokens(r, n):
            return pl.ds(n * T * dil + r, T, stride=dil) if dil > 1 else pl.ds(n * T, T)

        for bi, (r, n) in enumerate(blocks):
            s = _dot_nt(q_ref[0, 0, r, n * T:(n + 1) * T, :], k_ref[0, 0, r, keys(n), :]) * scale
            s_c = jnp.where(mask_c, s[:, -T:], MASK_VALUE)
            m = jnp.max(s_c, axis=-1, keepdims=True)
            s_scr[bi, :, T:] = s_c
            if n > 0:
                s_p = jnp.where(mask_p, s[:, :T], MASK_VALUE)
                m = jnp.maximum(m, jnp.max(s_p, axis=-1, keepdims=True))
                s_scr[bi, :, :T] = s_p
            m_scr[gi, tokens(r, n), :] = jnp.broadcast_to(m, (T, HEAD_DIM))
        for bi, (r, n) in enumerate(blocks):
            m = m_scr[gi, tokens(r, n), :]
            p_scr[bi, :, T:] = jnp.where(
                mask_c, jnp.exp(s_scr[bi, :, T:] - m), 0.0).astype(BF16)
            if n > 0:
                p_scr[bi, :, :T] = jnp.where(
                    mask_p, jnp.exp(s_scr[bi, :, :T] - m), 0.0).astype(BF16)
        for bi, (r, n) in enumerate(blocks):
            p = p_scr[bi] if n > 0 else p_scr[bi, :, T:]
            v = v_ref[0, 0, r, keys(n), :]
            o = _dot(p, jnp.concatenate([v, jnp.ones_like(v)], axis=1))
            o_scr[gi, tokens(r, n), :] = o[:, :HEAD_DIM]
            l_scr[gi, tokens(r, n), :] = o[:, HEAD_DIM:]

    for t in range(seq // T):
        rows = pl.ds(t * T, T)
        ms = [m_scr[gi, rows, :] for gi in range(n_g)]
        mx = functools.reduce(jnp.maximum, ms)
        ws = [jnp.exp(m - mx) for m in ms]
        num = sum(w * o_scr[gi, rows, :] for gi, w in enumerate(ws))
        den = sum(w * l_scr[gi, rows, :] for gi, w in enumerate(ws))
        out_ref[rows, :] = (num / den).astype(BF16)


def _attention(qs, ks, vs, batch, seq, n_heads):
    n_g = len(ATTN_GROUPS)
    in_specs, args = [], []
    for gi, (_, dil) in enumerate(ATTN_GROUPS):
        spec = pl.BlockSpec((1, 1, dil, seq // dil, HEAD_DIM), lambda b, h: (b, h, 0, 0, 0))
        in_specs += [spec, spec, spec]
        args += [qs[gi], ks[gi], vs[gi]]
    return pl.pallas_call(
        functools.partial(_attn_kernel, seq=seq),
        grid=(batch, n_heads),
        in_specs=in_specs,
        out_specs=pl.BlockSpec((seq, HEAD_DIM), lambda b, h: (b, h)),
        out_shape=jax.ShapeDtypeStruct((batch * seq, n_heads * HEAD_DIM), BF16),
        scratch_shapes=[pltpu.VMEM((seq // ATTN_BLOCK, ATTN_BLOCK, 2 * ATTN_BLOCK), F32),
                        pltpu.VMEM((seq // ATTN_BLOCK, ATTN_BLOCK, 2 * ATTN_BLOCK), BF16)]
        + [pltpu.VMEM((n_g, seq, HEAD_DIM), F32)] * 3,
        compiler_params=_params(("parallel", "parallel")),
        name="attn",
    )(*args)


def kernel(x, c, norm_g, w_ada, b_ada, w_ffn_in, w_ffn_out, hgrn_w_in, hgrn_w_out, hgrn_lb_logits, hgrn_out_gain, kv_norm_g, kv_w_ada, kv_b_ada, w_kv, k_gain, attn_w_q, attn_q_gain, attn_w_o):
    B, S, D = x.shape
    depth = w_ada.shape[0]
    n_a = hgrn_w_in.shape[0]
    n_groups = k_gain.shape[0]
    width = attn_w_o.shape[1]
    n_heads = width // HEAD_DIM
    GW = n_groups * width
    M = B * S
    assert n_groups == len(ATTN_GROUPS)
    assert all(win // dil == ATTN_BLOCK and S % (dil * ATTN_BLOCK) == 0 for win, dil in ATTN_GROUPS)

    mods = _ada(c, w_ada, b_ada).reshape(depth, B, 9, D)
    kv_mod = _ada(c, kv_w_ada[None], kv_b_ada[None]).reshape(B, 2, D)

    p = jax.nn.softmax(hgrn_lb_logits.astype(F32), axis=0)
    lb = jnp.cumsum(p, axis=0) - p[0]
    lb_floor = jnp.maximum(lb, LB_FLOOR)
    one_m_lb = 1 - lb
    w_ffn_in_b, w_ffn_out_b = w_ffn_in.astype(BF16), w_ffn_out.astype(BF16)
    hgrn_w_in_b, hgrn_w_out_b = hgrn_w_in.astype(BF16), hgrn_w_out.astype(BF16)
    attn_w_q_b, attn_w_o_b = attn_w_q.astype(BF16), attn_w_o.astype(BF16)
    w_kv_b = w_kv.astype(BF16)


    xf = x.reshape(M, D)
    ks = vs = None
    for l in range(depth):
        mod = mods[l]
        xf = _ffn(xf, norm_g[l, 0][None], mod, w_ffn_in_b, w_ffn_out_b, (l, 0), 0, S)
        if l < n_a:
            q, lf, k, v, sg = _hgrn_proj(xf, norm_g[l, 1][None], mod, hgrn_w_in_b, (l,),
                                         lb_floor[l][None], one_m_lb[l][None], S, 3, 4)
            a = _hgrn(q, lf, k, v, sg, hgrn_out_gain[l][None], S)
            xf = _oproj(a, hgrn_w_out_b, (l,), xf, mod, 5, S)
        else:
            jl = l - n_a
            (qs,) = _group_proj(xf, norm_g[l, 1][None], mod, attn_w_q_b, (jl,), [0],
                                attn_q_gain[jl], B, S, n_heads, 3, 4, "q_proj")
            a = _attention(qs, ks, vs, B, S, n_heads)
            xf = _oproj(a, attn_w_o_b, (jl,), xf, mod, 5, S)
        xf = _ffn(xf, norm_g[l, 2][None], mod, w_ffn_in_b, w_ffn_out_b, (l, 1), 2, S)
        if l == n_a - 1:
            ks, vs = _group_proj(xf, kv_norm_g[None], kv_mod, w_kv_b, (), [0, GW], k_gain,
                                 B, S, n_heads, 0, 1, "kv_proj")
    return xf.reshape(B, S, D)
```

```python
import functools

import numpy as np
import jax
import jax.numpy as jnp
from jax import lax
from jax.experimental import pallas as pl
from jax.experimental.pallas import tpu as pltpu

F32 = jnp.float32
BF16 = jnp.bfloat16

EPS = 1e-6
MASK_VALUE = -1e30
LB_FLOOR = 1e-30
LOG2_E = 1.4426950408889634
HEAD_DIM = 128
ATTN_GROUPS = ((128, 1), (512, 4), (2048, 16))
ATTN_BLOCK = 128
HGRN_CHUNK = 128
HGRN_DIAG = 2
HGRN_HEADS_PER_STEP = 4
HGRN_UNROLL = 4

LANES = 128
SUBLANES = 8
V7X_VMEM_BYTES = 64 * 1024 * 1024
VMEM_LIMIT_BYTES = V7X_VMEM_BYTES - 8 * 1024 * 1024
TOKEN_TILE = 512
COLUMN_TILE = 512
FFN_CHUNK = 512
WIDE_TILE = 1024


def _pick(n, target, mult=LANES):
    if n <= target:
        return n
    d = (target // mult) * mult
    while d >= mult:
        if n % d == 0:
            return d
        d -= mult
    raise ValueError(f"no tile for {n}")


def _params(sem):
    return pltpu.CompilerParams(dimension_semantics=sem, vmem_limit_bytes=VMEM_LIMIT_BYTES)


def _silu(x):
    return x * (0.5 + 0.5 * jnp.tanh(0.5 * x))


def _dot(a, b):
    return jnp.dot(a, b, preferred_element_type=F32)


def _dot_nt(a, b):
    return lax.dot_general(a, b, (((1,), (1,)), ((), ())), preferred_element_type=F32)


NORM_ROWS = 32


def _norm_mod_into(hn_ref, x_ref, gain, shift, scale):
    gs = gain * (1 + scale)
    for r in range(0, x_ref.shape[0], NORM_ROWS):
        x = x_ref[r:r + NORM_ROWS, :]
        ms = jnp.mean(x * x, axis=-1, keepdims=True)
        hn_ref[r:r + NORM_ROWS, :] = (x * lax.rsqrt(ms + EPS) * gs + shift).astype(BF16)


def _head_norm(a, gain):
    outs = []
    for h in range(a.shape[1] // HEAD_DIM):
        ah = a[:, h * HEAD_DIM:(h + 1) * HEAD_DIM]
        ms = jnp.mean(ah * ah, axis=-1, keepdims=True)
        outs.append(ah * lax.rsqrt(ms + EPS) * gain)
    return outs[0] if len(outs) == 1 else jnp.concatenate(outs, axis=1)


def _ada_kernel(c_ref, w_ref, b_ref, o_ref):
    sc = _silu(c_ref[...]).astype(BF16)
    o_ref[0] = _dot(sc, w_ref[0].astype(BF16)) + b_ref[0]


def _ada(c, w, b):
    L, D, N = w.shape
    B = c.shape[0]
    tn = _pick(N, WIDE_TILE)
    return pl.pallas_call(
        _ada_kernel,
        grid=(L, N // tn),
        in_specs=[
            pl.BlockSpec((B, D), lambda l, j: (0, 0)),
            pl.BlockSpec((1, D, tn), lambda l, j: (l, 0, j)),
            pl.BlockSpec((1, 1, tn), lambda l, j: (l, 0, j)),
        ],
        out_specs=pl.BlockSpec((1, B, tn), lambda l, j: (l, 0, j)),
        out_shape=jax.ShapeDtypeStruct((L, B, N), F32),
        compiler_params=_params(("parallel", "parallel")),
        name="ada",
    )(c, w, b.reshape(L, 1, N))


def _ffn_kernel(x_ref, g_ref, mod_ref, w_in_ref, w_out_ref, o_ref,
                hn_ref, wg_buf, wu_buf, wo_buf, sems, *, lead, sub, nj, tf, n_tiles):
    i = pl.program_id(0)
    F = nj * tf

    def copies(j, slot):
        cols = pl.ds(j * tf, tf)
        return (
            pltpu.make_async_copy(w_in_ref.at[lead + (slice(None), cols)], wg_buf.at[slot],
                                  sems.at[0, slot]),
            pltpu.make_async_copy(w_in_ref.at[lead + (slice(None), pl.ds(F + j * tf, tf))],
                                  wu_buf.at[slot], sems.at[1, slot]),
            pltpu.make_async_copy(w_out_ref.at[lead + (cols, slice(None))], wo_buf.at[slot],
                                  sems.at[2, slot]),
        )

    def start(j, slot):
        for c in copies(j, slot):
            c.start()

    @pl.when(i == 0)
    def _():
        start(0, 0)

    _norm_mod_into(hn_ref, x_ref, g_ref[...], mod_ref[0, 3 * sub:3 * sub + 1, :],
                   mod_ref[0, 3 * sub + 1:3 * sub + 2, :])
    hn = hn_ref[...]
    first = (i * nj) % 2 if nj % 2 else 0
    for j in range(nj):
        slot = (first + j) % 2
        if j + 1 < nj:
            start(j + 1, 1 - slot)
        else:
            @pl.when(i + 1 < n_tiles)
            def _():
                start(0, 1 - slot)
        for c in copies(j, slot):
            c.wait()
        hg = _dot(hn, wg_buf[slot])
        hu = _dot(hn, wu_buf[slot])
        act = (_silu(hg) * hu).astype(BF16)
        y = _dot(act, wo_buf[slot])
        if j == 0:
            o_ref[...] = y
        else:
            o_ref[...] += y
    gate = mod_ref[0, 3 * sub + 2:3 * sub + 3, :]
    o_ref[...] = x_ref[...] + (0.5 * (1 + gate)) * o_ref[...]


def _wspec(lead, block, index_map):
    return pl.BlockSpec((None,) * len(lead) + block, lambda i, j: lead + index_map(i, j))


def _ffn(x, gain, mod, w_in, w_out, lead, sub, seq):
    M, D = x.shape
    F = w_out.shape[-2]
    tm = _pick(seq, TOKEN_TILE, SUBLANES)
    tf = _pick(F, FFN_CHUNK)
    nj = F // tf
    per_b = seq // tm
    return pl.pallas_call(
        functools.partial(_ffn_kernel, lead=lead, sub=sub, nj=nj, tf=tf, n_tiles=M // tm),
        grid=(M // tm,),
        in_specs=[
            pl.BlockSpec((tm, D), lambda i: (i, 0)),
            pl.BlockSpec((1, D), lambda i: (0, 0)),
            pl.BlockSpec((1, 9, D), lambda i: (i // per_b, 0, 0)),
            pl.BlockSpec(memory_space=pl.ANY),
            pl.BlockSpec(memory_space=pl.ANY),
        ],
        out_specs=pl.BlockSpec((tm, D), lambda i: (i, 0)),
        out_shape=jax.ShapeDtypeStruct((M, D), F32),
        scratch_shapes=[pltpu.VMEM((tm, D), BF16),
                        pltpu.VMEM((2, D, tf), BF16), pltpu.VMEM((2, D, tf), BF16),
                        pltpu.VMEM((2, tf, D), BF16), pltpu.SemaphoreType.DMA((3, 2))],
        compiler_params=_params(("arbitrary",)),
        name="ffn",
    )(x, gain, mod, w_in, w_out)


def _hgrn_proj_kernel(x_ref, g_ref, mod_ref, wf_ref, wq_ref, wg_ref, wv_ref, lbf_ref, oml_ref,
                      q_ref, lf_ref, k_ref, v_ref, sg_ref, hn_ref, *, shift_row, scale_row):
    @pl.when(pl.program_id(1) == 0)
    def _():
        _norm_mod_into(hn_ref, x_ref, g_ref[...], mod_ref[0, shift_row:shift_row + 1, :],
                       mod_ref[0, scale_row:scale_row + 1, :])

    hn = hn_ref[...]
    ff, q, g, v = (_dot(hn, w[...]) for w in (wf_ref, wq_ref, wg_ref, wv_ref))
    e = jnp.exp(-jnp.abs(ff))
    r = 1.0 / (1.0 + e)
    pos = ff >= 0
    one_m_lb = oml_ref[...]
    lf_ref[...] = jnp.log(lbf_ref[...] + one_m_lb * jnp.where(pos, r, e * r)) * LOG2_E
    k_ref[...] = one_m_lb * jnp.where(pos, e * r, r)
    q_ref[...] = _silu(q)
    sg_ref[...] = _silu(g).astype(BF16)
    v_ref[...] = v.astype(BF16)


def _hgrn_proj(x, gain, mod, w, lead, lb_floor, one_m_lb, seq, shift_row, scale_row):
    M, D = x.shape
    tm = _pick(seq, TOKEN_TILE, SUBLANES)
    tn = _pick(D, COLUMN_TILE)
    per_b = seq // tm
    section = lambda k: _wspec(lead, (D, tn), lambda i, j: (0, k * (D // tn) + j))
    vec = lambda: pl.BlockSpec((1, tn), lambda i, j: (0, j))
    tile = lambda: pl.BlockSpec((tm, tn), lambda i, j: (i, j))
    return pl.pallas_call(
        functools.partial(_hgrn_proj_kernel, shift_row=shift_row, scale_row=scale_row),
        grid=(M // tm, D // tn),
        in_specs=[
            pl.BlockSpec((tm, D), lambda i, j: (i, 0)),
            pl.BlockSpec((1, D), lambda i, j: (0, 0)),
            pl.BlockSpec((1, mod.shape[1], D), lambda i, j: (i // per_b, 0, 0)),
            section(1), section(0), section(3), section(2), vec(), vec(),
        ],
        out_specs=[tile() for _ in range(5)],
        out_shape=[jax.ShapeDtypeStruct((M, D), dt) for dt in (F32, F32, F32, BF16, BF16)],
        scratch_shapes=[pltpu.VMEM((tm, D), BF16)],
        compiler_params=_params(("parallel", "arbitrary")),
        name="hgrn_proj",
    )(x, gain, mod, w, w, w, w, lb_floor, one_m_lb)


def _store_dilated(a, scr_ref, out_ref, dil):
    rows, width = a.shape
    for h in range(width // HEAD_DIM):
        ah = a[:, h * HEAD_DIM:(h + 1) * HEAD_DIM]
        if dil == 1:
            out_ref[0, h, 0] = ah.astype(BF16)
        else:
            scr_ref[h] = ah
            for r in range(dil):
                out_ref[0, h, r] = scr_ref[h, pl.ds(r, rows // dil, stride=dil), :].astype(BF16)


def _group_proj_kernel(x_ref, g_ref, mod_ref, gain_ref, w_ref, *refs, lead, col_offsets, width,
                       shift_row, scale_row, n_tiles):
    n_g, n_w = len(ATTN_GROUPS), len(col_offsets)
    out_refs = refs[:n_w * n_g]
    hn_ref, w_buf, scr_ref, sems = refs[n_w * n_g:]
    i = pl.program_id(0)
    order = sorted(range(n_g), key=lambda gi: -ATTN_GROUPS[gi][1])

    def copies(c, slot):
        return [pltpu.make_async_copy(
            w_ref.at[lead + (slice(None), pl.ds(off + order[c] * width, width))],
            w_buf.at[slot, k], sems.at[k, slot]) for k, off in enumerate(col_offsets)]

    def start(c, slot):
        for cp in copies(c, slot):
            cp.start()

    @pl.when(i == 0)
    def _():
        start(0, 0)

    _norm_mod_into(hn_ref, x_ref, g_ref[...], mod_ref[0, shift_row:shift_row + 1, :],
                   mod_ref[0, scale_row:scale_row + 1, :])
    hn = hn_ref[...]
    first = (i * n_g) % 2 if n_g % 2 else 0
    for c, gi in enumerate(order):
        slot = (first + c) % 2
        if c + 1 < n_g:
            start(c + 1, 1 - slot)
        else:
            @pl.when(i + 1 < n_tiles)
            def _():
                start(0, 1 - slot)
        for cp in copies(c, slot):
            cp.wait()
        dil = ATTN_GROUPS[gi][1]
        for k in range(n_w):
            a = _dot(hn, w_buf[slot, k])
            if k == 0:
                a = _head_norm(a, gain_ref[gi])
            _store_dilated(a, scr_ref.at[k], out_refs[k * n_g + gi], dil)


def _group_proj(x, gain, mod, w, lead, col_offsets, head_gain, batch, seq, n_heads,
                shift_row, scale_row, name):
    M, D = x.shape
    n_g, n_w = len(ATTN_GROUPS), len(col_offsets)
    width = n_heads * HEAD_DIM
    tm = _pick(seq, TOKEN_TILE, SUBLANES * max(d for _, d in ATTN_GROUPS))
    per_b = seq // tm
    n_mod = mod.shape[1]
    out_specs, out_shapes = [], []
    for _ in range(n_w):
        for _, dil in ATTN_GROUPS:
            out_specs.append(pl.BlockSpec((1, n_heads, dil, tm // dil, HEAD_DIM),
                                          lambda i: (i // per_b, 0, 0, i % per_b, 0)))
            out_shapes.append(
                jax.ShapeDtypeStruct((batch, n_heads, dil, seq // dil, HEAD_DIM), BF16))
    outs = pl.pallas_call(
        functools.partial(_group_proj_kernel, lead=lead, col_offsets=tuple(col_offsets),
                          width=width, shift_row=shift_row, scale_row=scale_row,
                          n_tiles=M // tm),
        grid=(M // tm,),
        in_specs=[
            pl.BlockSpec((tm, D), lambda i: (i, 0)),
            pl.BlockSpec((1, D), lambda i: (0, 0)),
            pl.BlockSpec((1, n_mod, D), lambda i: (i // per_b, 0, 0)),
            pl.BlockSpec((n_g, 1, HEAD_DIM), lambda i: (0, 0, 0)),
            pl.BlockSpec(memory_space=pl.ANY),
        ],
        out_specs=out_specs,
        out_shape=out_shapes,
        scratch_shapes=[pltpu.VMEM((tm, D), BF16), pltpu.VMEM((2, n_w, D, width), BF16),
                        pltpu.VMEM((n_w, n_heads, tm, HEAD_DIM), F32),
                        pltpu.SemaphoreType.DMA((n_w, 2))],
        compiler_params=_params(("arbitrary",)),
        name=name,
    )(x, gain, mod, head_gain[:, None, :], w)
    return [outs[k * n_g:(k + 1) * n_g] for k in range(n_w)]


def _hgrn_levels():
    P, out = 2 * HGRN_DIAG, []
    while P <= HGRN_CHUNK:
        out.append(P)
        P *= 2
    return out


def _hgrn_masks():
    t = np.arange(HGRN_CHUNK)
    rows = []
    for P in _hgrn_levels():
        same = (t[:, None] // P) == (t[None, :] // P)
        rows.append(same & ((t[:, None] % P) >= P // 2) & ((t[None, :] % P) < P // 2))
    return jnp.asarray(np.stack(rows), F32)


def _dot_tn(a, b):
    return lax.dot_general(a, b, (((0,), (0,)), ((), ())), preferred_element_type=F32)


def _hgrn_chunk(qs, lf, k, v, ST, tri, ones, delta, same_diag_block, tok, mask_ref):
    C, c = HGRN_CHUNK, HGRN_DIAG
    lf_hi = lf.astype(BF16)
    lf_lo = (lf - lf_hi.astype(F32)).astype(BF16)
    G = _dot(tri, lf_hi) + _dot(tri, lf_lo)

    W = [(qs * k).astype(BF16)]
    for d in range(1, c):
        ks = pltpu.roll(k, d, axis=0)
        Gs = pltpu.roll(G, d, axis=0)
        W.append((qs * ks * jnp.exp2(G - Gs)).astype(BF16))
    R = _dot(jnp.concatenate(W, axis=0), ones)
    A = jnp.zeros((C, C), F32)
    for d in range(c):
        A = jnp.where(delta == d, R[d * C:(d + 1) * C], A)
    A = jnp.where(same_diag_block, A, 0.0)

    for li, P in enumerate(_hgrn_levels()):
        half = P // 2
        if half % 8 == 0:
            src, expo = [], []
            for b in range(C // P):
                lo, mid, hi = b * P, b * P + half, (b + 1) * P
                g_mid = G[mid - 1:mid, :]
                src += [k[lo:mid], qs[mid:hi]]
                expo += [g_mid - G[lo:mid], G[mid:hi] - g_mid]
            Z = jnp.concatenate(src, axis=0) * jnp.exp2(jnp.concatenate(expo, axis=0))
        else:
            Ge = jnp.concatenate(
                [jnp.broadcast_to(G[b * P + half - 1:b * P + half, :], (P, HEAD_DIM))
                 for b in range(C // P)], axis=0)
            second = (tok % P) >= half
            Z = jnp.where(second, qs, k) * jnp.exp2(jnp.where(second, 1.0, -1.0) * (G - Ge))
        Z = Z.astype(BF16)
        A = A + mask_ref[li] * _dot_nt(Z, Z)

    o = _dot(A.astype(BF16), v) + _dot_nt((qs * jnp.exp2(G)).astype(BF16), ST.astype(BF16))
    g_last = G[C - 1:C, :]
    k_dec = (k * jnp.exp2(g_last - G)).astype(BF16)
    ST_new = ST * jnp.exp2(g_last) + _dot_tn(v, k_dec)
    return o, ST_new


def _hgrn_kernel(q_ref, lf_ref, k_ref, v_ref, sg_ref, gain_ref, mask_ref, o_ref, st_ref,
                 *, n_chunks, n_heads):
    C, c = HGRN_CHUNK, HGRN_DIAG
    row = lax.broadcasted_iota(jnp.int32, (C, C), 0)
    col = lax.broadcasted_iota(jnp.int32, (C, C), 1)
    tri = (col <= row).astype(BF16)
    ones = jnp.ones((HEAD_DIM, C), BF16)
    delta = row - col
    same_diag_block = (row // c) == (col // c)
    tok = lax.broadcasted_iota(jnp.int32, (C, 1), 0)
    gain = gain_ref[...]
    st_ref[...] = jnp.zeros_like(st_ref)

    def chunk(ci, carry):
        rows = pl.ds(pl.multiple_of(ci * C, C), C)
        for h in range(n_heads):
            sl = slice(h * HEAD_DIM, (h + 1) * HEAD_DIM)
            o, S_new = _hgrn_chunk(q_ref[rows, sl], lf_ref[rows, sl], k_ref[rows, sl],
                                   v_ref[rows, sl], st_ref[h], tri, ones, delta,
                                   same_diag_block, tok, mask_ref)
            st_ref[h] = S_new
            ms = jnp.mean(o * o, axis=-1, keepdims=True)
            on = o * lax.rsqrt(ms + EPS) * gain
            o_ref[rows, sl] = (on * sg_ref[rows, sl].astype(F32)).astype(BF16)
        return carry

    lax.fori_loop(0, n_chunks, chunk, 0, unroll=HGRN_UNROLL)


def _hgrn(q, lf, k, v, sg, out_gain, seq):
    M, D = q.shape
    hb = min(HGRN_HEADS_PER_STEP, D // HEAD_DIM)
    masks = _hgrn_masks()
    blk = lambda: pl.BlockSpec((seq, hb * HEAD_DIM), lambda b, h: (b, h))
    return pl.pallas_call(
        functools.partial(_hgrn_kernel, n_chunks=seq // HGRN_CHUNK, n_heads=hb),
        grid=(M // seq, D // (hb * HEAD_DIM)),
        in_specs=[blk(), blk(), blk(), blk(), blk(),
                  pl.BlockSpec((1, HEAD_DIM), lambda b, h: (0, 0)),
                  pl.BlockSpec(masks.shape, lambda b, h: (0, 0, 0))],
        out_specs=blk(),
        out_shape=jax.ShapeDtypeStruct((M, D), BF16),
        scratch_shapes=[pltpu.VMEM((hb, HEAD_DIM, HEAD_DIM), F32)],
        compiler_params=_params(("parallel", "parallel")),
        name="hgrn",
    )(q, lf, k, v, sg, out_gain, masks)


def _oproj_kernel(a_ref, w_ref, x_ref, mod_ref, o_ref, *, gate_row):
    gate = mod_ref[0, gate_row:gate_row + 1, :]
    o_ref[...] = x_ref[...] + (1 + gate) * _dot(a_ref[...], w_ref[...])


def _oproj(a, w, lead, x, mod, gate_row, seq):
    M, K = a.shape
    D = w.shape[-1]
    tm = _pick(seq, TOKEN_TILE, SUBLANES)
    tn = D
    per_b = seq // tm
    return pl.pallas_call(
        functools.partial(_oproj_kernel, gate_row=gate_row),
        grid=(M // tm, D // tn),
        in_specs=[
            pl.BlockSpec((tm, K), lambda i, j: (i, 0)),
            _wspec(lead, (K, tn), lambda i, j: (0, j)),
            pl.BlockSpec((tm, tn), lambda i, j: (i, j)),
            pl.BlockSpec((1, 9, tn), lambda i, j: (i // per_b, 0, j)),
        ],
        out_specs=pl.BlockSpec((tm, tn), lambda i, j: (i, j)),
        out_shape=jax.ShapeDtypeStruct((M, D), F32),
        compiler_params=_params(("parallel", "arbitrary")),
        name="oproj",
    )(a, w, x, mod)


def _attn_kernel(*refs, seq):
    n_g = len(ATTN_GROUPS)
    qkv = refs[:3 * n_g]
    out_ref = refs[3 * n_g]
    s_scr, p_scr, o_scr, m_scr, l_scr = refs[3 * n_g + 1:]
    T = ATTN_BLOCK
    scale = HEAD_DIM ** -0.5
    row = lax.broadcasted_iota(jnp.int32, (T, T), 0)
    col = lax.broadcasted_iota(jnp.int32, (T, T), 1)
    mask_c = col <= row
    mask_p = col >= row

    for gi, (_, dil) in enumerate(ATTN_GROUPS):
        q_ref, k_ref, v_ref = qkv[3 * gi:3 * gi + 3]
        blocks = [(r, n) for r in range(dil) for n in range(seq // dil // T)]

        def keys(n):
            return slice(max(n - 1, 0) * T, (n + 1) * T)

        def tokens(r, n):
            return pl.ds(n * T * dil + r, T, stride=dil) if dil > 1 else pl.ds(n * T, T)

        for bi, (r, n) in enumerate(blocks):
            s = _dot_nt(q_ref[0, 0, r, n * T:(n + 1) * T, :], k_ref[0, 0, r, keys(n), :]) * scale
            s_c = jnp.where(mask_c, s[:, -T:], MASK_VALUE)
            m = jnp.max(s_c, axis=-1, keepdims=True)
            s_scr[bi, :, T:] = s_c
            if n > 0:
                s_p = jnp.where(mask_p, s[:, :T], MASK_VALUE)
                m = jnp.maximum(m, jnp.max(s_p, axis=-1, keepdims=True))
                s_scr[bi, :, :T] = s_p
            m_scr[gi, tokens(r, n), :] = jnp.broadcast_to(m, (T, HEAD_DIM))
        for bi, (r, n) in enumerate(blocks):
            m = m_scr[gi, tokens(r, n), :]
            p_scr[bi, :, T:] = jnp.where(
                mask_c, jnp.exp(s_scr[bi, :, T:] - m), 0.0).astype(BF16)
            if n > 0:
                p_scr[bi, :, :T] = jnp.where(
                    mask_p, jnp.exp(s_scr[bi, :, :T] - m), 0.0).astype(BF16)
        for bi, (r, n) in enumerate(blocks):
            p = p_scr[bi] if n > 0 else p_scr[bi, :, T:]
            v = v_ref[0, 0, r, keys(n), :]
            o = _dot(p, jnp.concatenate([v, jnp.ones_like(v)], axis=1))
            o_scr[gi, tokens(r, n), :] = o[:, :HEAD_DIM]
            l_scr[gi, tokens(r, n), :] = o[:, HEAD_DIM:]

    for t in range(seq // T):
        rows = pl.ds(t * T, T)
        ms = [m_scr[gi, rows, :] for gi in range(n_g)]
        mx = functools.reduce(jnp.maximum, ms)
        ws = [jnp.exp(m - mx) for m in ms]
        num = sum(w * o_scr[gi, rows, :] for gi, w in enumerate(ws))
        den = sum(w * l_scr[gi, rows, :] for gi, w in enumerate(ws))
        out_ref[rows, :] = (num / den).astype(BF16)


def _attention(qs, ks, vs, batch, seq, n_heads):
    n_g = len(ATTN_GROUPS)
    in_specs, args = [], []
    for gi, (_, dil) in enumerate(ATTN_GROUPS):
        spec = pl.BlockSpec((1, 1, dil, seq // dil, HEAD_DIM), lambda b, h: (b, h, 0, 0, 0))
        in_specs += [spec, spec, spec]
        args += [qs[gi], ks[gi], vs[gi]]
    return pl.pallas_call(
        functools.partial(_attn_kernel, seq=seq),
        grid=(batch, n_heads),
        in_specs=in_specs,
        out_specs=pl.BlockSpec((seq, HEAD_DIM), lambda b, h: (b, h)),
        out_shape=jax.ShapeDtypeStruct((batch * seq, n_heads * HEAD_DIM), BF16),
        scratch_shapes=[pltpu.VMEM((seq // ATTN_BLOCK, ATTN_BLOCK, 2 * ATTN_BLOCK), F32),
                        pltpu.VMEM((seq // ATTN_BLOCK, ATTN_BLOCK, 2 * ATTN_BLOCK), BF16)]
        + [pltpu.VMEM((n_g, seq, HEAD_DIM), F32)] * 3,
        compiler_params=_params(("parallel", "parallel")),
        name="attn",
    )(*args)


def kernel(x, c, norm_g, w_ada, b_ada, w_ffn_in, w_ffn_out, hgrn_w_in, hgrn_w_out, hgrn_lb_logits, hgrn_out_gain, kv_norm_g, kv_w_ada, kv_b_ada, w_kv, k_gain, attn_w_q, attn_q_gain, attn_w_o):
    B, S, D = x.shape
    depth = w_ada.shape[0]
    n_a = hgrn_w_in.shape[0]
    n_groups = k_gain.shape[0]
    width = attn_w_o.shape[1]
    n_heads = width // HEAD_DIM
    GW = n_groups * width
    M = B * S
    assert n_groups == len(ATTN_GROUPS)
    assert all(win // dil == ATTN_BLOCK and S % (dil * ATTN_BLOCK) == 0 for win, dil in ATTN_GROUPS)

    mods = _ada(c, w_ada, b_ada).reshape(depth, B, 9, D)
    kv_mod = _ada(c, kv_w_ada[None], kv_b_ada[None]).reshape(B, 2, D)

    p = jax.nn.softmax(hgrn_lb_logits.astype(F32), axis=0)
    lb = jnp.cumsum(p, axis=0) - p[0]
    lb_floor = jnp.maximum(lb, LB_FLOOR)
    one_m_lb = 1 - lb
    w_ffn_in_b, w_ffn_out_b = w_ffn_in.astype(BF16), w_ffn_out.astype(BF16)
    hgrn_w_in_b, hgrn_w_out_b = hgrn_w_in.astype(BF16), hgrn_w_out.astype(BF16)
    attn_w_q_b, attn_w_o_b = attn_w_q.astype(BF16), attn_w_o.astype(BF16)
    w_kv_b = w_kv.astype(BF16)


    xf = x.reshape(M, D)
    ks = vs = None
    for l in range(depth):
        mod = mods[l]
        xf = _ffn(xf, norm_g[l, 0][None], mod, w_ffn_in_b, w_ffn_out_b, (l, 0), 0, S)
        if l < n_a:
            q, lf, k, v, sg = _hgrn_proj(xf, norm_g[l, 1][None], mod, hgrn_w_in_b, (l,),
                                         lb_floor[l][None], one_m_lb[l][None], S, 3, 4)
            a = _hgrn(q, lf, k, v, sg, hgrn_out_gain[l][None], S)
            xf = _oproj(a, hgrn_w_out_b, (l,), xf, mod, 5, S)
        else:
            jl = l - n_a
            (qs,) = _group_proj(xf, norm_g[l, 1][None], mod, attn_w_q_b, (jl,), [0],
                                attn_q_gain[jl], B, S, n_heads, 3, 4, "q_proj")
            a = _attention(qs, ks, vs, B, S, n_heads)
            xf = _oproj(a, attn_w_o_b, (jl,), xf, mod, 5, S)
        xf = _ffn(xf, norm_g[l, 2][None], mod, w_ffn_in_b, w_ffn_out_b, (l, 1), 2, S)
        if l == n_a - 1:
            ks, vs = _group_proj(xf, kv_norm_g[None], kv_mod, w_kv_b, (), [0, GW], k_gain,
                                 B, S, n_heads, 0, 1, "kv_proj")
    return xf.reshape(B, S, D)
```
